```python
import math
import jax, jax.numpy as jnp
from jax import lax
import numpy as np


D_MODEL = 2048
BATCH = 8
SEQ = 2048
DEPTH = 2

GDN_HEADS = 8
GDN_DK = 128
GDN_DV = 128
CONV_K = 4
GDN_CHUNK = 64
GDN_CONV_CH = GDN_HEADS * (2 * GDN_DK + GDN_DV)
GLA_HEADS = 4
GLA_DK = 128
GLA_DV = 256
GLA_RANK = 16
GLA_NORMALIZER = 16.0
GLA_CHUNK = 64
SB_HEADS = 8
SB_DH = 128
SB_BLOCK = 128
N_BRANCH = 3
BRANCH_WIDTH = 1024
D_FF = 7168
N_EXPERTS = 8
TOP_K = 2
MOE_BLOCK = 128
N_DENSE = (DEPTH + 1) // 2
N_MOE = DEPTH // 2
EPS = 1e-6

IN_SIZES = (
    GDN_HEADS * GDN_DK, GDN_HEADS * GDN_DK, GDN_HEADS * GDN_DV,
    GDN_HEADS * GDN_DV, GDN_HEADS, GDN_HEADS,
    GLA_HEADS * GLA_DK, GLA_HEADS * GLA_DK, GLA_HEADS * GLA_DV,
    GLA_HEADS * GLA_DV, GLA_RANK,
    SB_HEADS * SB_DH, SB_HEADS * SB_DH, SB_HEADS * SB_DH,
    N_BRANCH * D_MODEL,
)
D_IN = sum(IN_SIZES)

kernel_name = 'hybrid_gdn_gla_stickbreak_moe'


def rms_norm(x, gain):
    xf = x.astype(jnp.float32)
    y = xf * lax.rsqrt(jnp.mean(xf * xf, axis=-1, keepdims=True) + EPS)
    return (y * gain.astype(jnp.float32)).astype(x.dtype)


def l2_norm(x):
    return x * lax.rsqrt(jnp.sum(x * x, axis=-1, keepdims=True) + EPS)


def causal_dwconv(x, w):
    c = x.shape[-1]
    return lax.conv_general_dilated(
        x, w[:, None, :].astype(x.dtype), window_strides=(1,), padding=[(CONV_K - 1, 0)],
        dimension_numbers=('NWC', 'WIO', 'NWC'), feature_group_count=c)


def to_chunks(x, heads, chunk):
    b, s, _ = x.shape
    return x.reshape(b, s // chunk, chunk, heads, -1).transpose(0, 3, 1, 2, 4)


def head_chunks(x, chunk):
    b, s, h = x.shape
    return x.reshape(b, s // chunk, chunk, h).transpose(0, 3, 1, 2)


def from_chunks(o):
    b, h, n, c, d = o.shape
    return o.transpose(0, 2, 3, 1, 4).reshape(b, n * c, h, d)


def gated_delta_rule(q, k, v, log_decay, beta):
    b, h, n, c, dk = q.shape
    dv = v.shape[-1]
    q = q * dk ** -0.5
    g = jnp.cumsum(log_decay, axis=-1)
    causal = jnp.tril(jnp.ones((c, c), dtype=bool))
    strict = jnp.tril(jnp.ones((c, c), dtype=bool), k=-1)
    decay = jnp.exp(jnp.where(causal, g[..., :, None] - g[..., None, :], -jnp.inf))
    k_beta = k * beta[..., None]
    l_mat = jnp.where(strict, jnp.einsum('bhnid,bhnjd->bhnij', k_beta, k) * decay, 0.0)
    rhs = jnp.concatenate([v * beta[..., None], k_beta * jnp.exp(g)[..., None]], axis=-1)
    sol = lax.linalg.triangular_solve(l_mat + jnp.eye(c, dtype=q.dtype), rhs, left_side=True, lower=True)
    u, w = sol[..., :dv], sol[..., dv:]
    qk = jnp.einsum('bhnid,bhnjd->bhnij', q, k) * decay
    q_dec = q * jnp.exp(g)[..., None]
    k_dec = k * jnp.exp(g[..., -1:] - g)[..., None]
    g_last = jnp.exp(g[..., -1])

    def step(state, inp):
        u_n, w_n, qk_n, qd_n, kd_n, gl_n = inp
        v_new = u_n - jnp.einsum('bhcd,bhde->bhce', w_n, state)
        o = jnp.einsum('bhcd,bhde->bhce', qd_n, state) + jnp.einsum('bhij,bhje->bhie', qk_n, v_new)
        state = state * gl_n[..., None, None] + jnp.einsum('bhcd,bhce->bhde', kd_n, v_new)
        return state, o

    xs = (jnp.moveaxis(u, 2, 0), jnp.moveaxis(w, 2, 0), jnp.moveaxis(qk, 2, 0),
          jnp.moveaxis(q_dec, 2, 0), jnp.moveaxis(k_dec, 2, 0), jnp.moveaxis(g_last, 2, 0))
    s0 = jnp.zeros((b, h, dk, dv), q.dtype)
    _, o = lax.scan(step, s0, xs)
    return jnp.moveaxis(o, 0, 2)


def gla_chunked(q, k, v, log_g):
    b, h, n, c, dk = q.shape
    dv = v.shape[-1]
    q = q * dk ** -0.5
    G = jnp.cumsum(log_g, axis=-2)
    ref = G[..., c // 2 - 1:c // 2, :]
    causal = jnp.tril(jnp.ones((c, c), dtype=bool))
    a_intra = jnp.einsum('bhnid,bhnjd->bhnij', q * jnp.exp(G - ref), k * jnp.exp(ref - G))
    a_intra = jnp.where(causal, a_intra, 0.0)
    q_dec = q * jnp.exp(G)
    k_dec = k * jnp.exp(G[..., -1:, :] - G)
    g_last = jnp.exp(G[..., -1, :])

    def step(state, inp):
        a_n, v_n, qd_n, kd_n, gl_n = inp
        o = jnp.einsum('bhcd,bhde->bhce', qd_n, state) + jnp.einsum('bhij,bhje->bhie', a_n, v_n)
        state = state * gl_n[..., :, None] + jnp.einsum('bhcd,bhce->bhde', kd_n, v_n)
        return state, o

    xs = (jnp.moveaxis(a_intra, 2, 0), jnp.moveaxis(v, 2, 0), jnp.moveaxis(q_dec, 2, 0),
          jnp.moveaxis(k_dec, 2, 0), jnp.moveaxis(g_last, 2, 0))
    s0 = jnp.zeros((b, h, dk, dv), q.dtype)
    _, o = lax.scan(step, s0, xs)
    return jnp.moveaxis(o, 0, 2)


def stick_breaking_attention(q, k, v):
    s_len = q.shape[2]
    scale = SB_DH ** -0.5
    outs = []
    for blk in range(s_len // SB_BLOCK):
        q0 = blk * SB_BLOCK
        n_keys = q0 + SB_BLOCK
        z = jnp.einsum('bhtd,bhsd->bhts', q[:, :, q0:n_keys], k[:, :, :n_keys]) * scale
        t_idx = q0 + jnp.arange(SB_BLOCK)[:, None]
        s_idx = jnp.arange(n_keys)[None, :]
        strict = s_idx < t_idx
        log_beta = jax.nn.log_sigmoid(z)
        log_1m = jnp.where(strict, log_beta - z, 0.0)
        tail = lax.cumsum(log_1m, axis=3, reverse=True) - log_1m
        attn = jnp.where(strict, jnp.exp(log_beta + tail), 0.0)
        outs.append(jnp.einsum('bhts,bhsd->bhtd', attn, v[:, :, :n_keys]))
    return jnp.concatenate(outs, axis=2)


def token_mixers(xn, w_in, conv_w, gdn_a_log, gdn_dt_bias, gdn_norm, gla_w_gate, gla_b_gate,
                 gla_norm, sb_q_norm, sb_k_norm, w_branch, w_out):
    b, s, _ = xn.shape
    f32 = jnp.float32
    offsets = [int(o) for o in np.cumsum(IN_SIZES)[:-1]]
    proj = xn @ w_in
    (a_q, a_k, a_v, a_z, a_a, a_b, b_q, b_k, b_v, b_r, b_g,
     c_q, c_k, c_v, gates) = jnp.split(proj, offsets, axis=-1)

    qkv = jax.nn.silu(causal_dwconv(jnp.concatenate([a_q, a_k, a_v], axis=-1), conv_w))
    a_q, a_k, a_v = jnp.split(qkv, [GDN_HEADS * GDN_DK, 2 * GDN_HEADS * GDN_DK], axis=-1)
    gq = l2_norm(to_chunks(a_q, GDN_HEADS, GDN_CHUNK).astype(f32))
    gk = l2_norm(to_chunks(a_k, GDN_HEADS, GDN_CHUNK).astype(f32))
    gv = to_chunks(a_v, GDN_HEADS, GDN_CHUNK).astype(f32)
    log_decay = -jnp.exp(gdn_a_log.astype(f32)) * jax.nn.softplus(a_a.astype(f32) + gdn_dt_bias.astype(f32))
    beta = jax.nn.sigmoid(a_b.astype(f32))
    o_a = from_chunks(gated_delta_rule(gq, gk, gv, head_chunks(log_decay, GDN_CHUNK),
                                       head_chunks(beta, GDN_CHUNK)))
    o_a = rms_norm(o_a, gdn_norm) * jax.nn.silu(a_z.reshape(b, s, GDN_HEADS, GDN_DV).astype(f32))
    y_a = o_a.reshape(b, s, BRANCH_WIDTH).astype(xn.dtype)

    log_g = jax.nn.log_sigmoid((b_g @ gla_w_gate + gla_b_gate).astype(f32)) / GLA_NORMALIZER
    o_b = from_chunks(gla_chunked(to_chunks(b_q, GLA_HEADS, GLA_CHUNK).astype(f32),
                                  to_chunks(b_k, GLA_HEADS, GLA_CHUNK).astype(f32),
                                  to_chunks(b_v, GLA_HEADS, GLA_CHUNK).astype(f32),
                                  to_chunks(log_g, GLA_HEADS, GLA_CHUNK)))
    o_b = rms_norm(o_b, gla_norm) * jax.nn.silu(b_r.reshape(b, s, GLA_HEADS, GLA_DV).astype(f32))
    y_b = o_b.reshape(b, s, BRANCH_WIDTH).astype(xn.dtype)

    sq = rms_norm(c_q.reshape(b, s, SB_HEADS, SB_DH), sb_q_norm).astype(f32).transpose(0, 2, 1, 3)
    sk = rms_norm(c_k.reshape(b, s, SB_HEADS, SB_DH), sb_k_norm).astype(f32).transpose(0, 2, 1, 3)
    sv = c_v.reshape(b, s, SB_HEADS, SB_DH).astype(f32).transpose(0, 2, 1, 3)
    o_c = stick_breaking_attention(sq, sk, sv)
    y_c = o_c.transpose(0, 2, 1, 3).reshape(b, s, BRANCH_WIDTH).astype(xn.dtype)

    ys = jnp.stack([y_a, y_b, y_c], axis=0)
    branch = jnp.einsum('nbsw,nwd->bsnd', ys, w_branch)
    gate = jax.nn.sigmoid(gates.reshape(b, s, N_BRANCH, D_MODEL))
    merged = jnp.sum(gate * branch, axis=2)
    return merged @ w_out


def swiglu(xn, w1, w3, w2):
    return (jax.nn.silu(xn @ w1) * (xn @ w3)) @ w2


def moe_swiglu(xn, w_router, w1, w3, w2):
    b, s, d = xn.shape
    n_tok = b * s
    xt = xn.reshape(n_tok, d)
    logits = (xt @ w_router).astype(jnp.float32)
    top_logit, top_idx = lax.top_k(logits, TOP_K)
    top_w = jax.nn.softmax(top_logit, axis=-1)
    n_assign = n_tok * TOP_K
    expert_flat = top_idx.reshape(-1)
    token_flat = jnp.repeat(jnp.arange(n_tok, dtype=jnp.int32), TOP_K)
    weight_flat = top_w.reshape(-1)
    order = jnp.argsort(expert_flat)
    e_sorted = expert_flat[order]
    t_sorted = token_flat[order]
    w_sorted = weight_flat[order]
    counts = jnp.bincount(expert_flat, length=N_EXPERTS)
    padded = (counts + MOE_BLOCK - 1) // MOE_BLOCK * MOE_BLOCK
    start = jnp.cumsum(counts) - counts
    ends_p = jnp.cumsum(padded)
    start_p = ends_p - padded
    dest = start_p[e_sorted] + jnp.arange(n_assign, dtype=jnp.int32) - start[e_sorted]
    n_blocks = -(-n_assign // MOE_BLOCK) + N_EXPERTS
    buf = n_blocks * MOE_BLOCK
    tok_buf = jnp.zeros((buf,), jnp.int32).at[dest].set(t_sorted)
    w_buf = jnp.zeros((buf,), jnp.float32).at[dest].set(w_sorted)
    block_expert = jnp.minimum(
        jnp.searchsorted(ends_p, jnp.arange(n_blocks, dtype=jnp.int32) * MOE_BLOCK, side='right'),
        N_EXPERTS - 1)

    def block_ffn(args):
        tok, e = args
        xb = xt[tok]
        return (jax.nn.silu(xb @ w1[e]) * (xb @ w3[e])) @ w2[e]

    yb = lax.map(block_ffn, (tok_buf.reshape(n_blocks, MOE_BLOCK), block_expert))
    contrib = (yb.reshape(buf, d) * w_buf[:, None]).astype(xn.dtype)
    out = jnp.zeros((n_tok, d), xn.dtype).at[tok_buf].add(contrib)
    return out.reshape(b, s, d)


def setup_inputs(seed: int = 0) -> dict:
    key = jax.random.key(seed)
    ks = jax.random.split(key, 24)
    f32 = jnp.float32

    def nrm(k, shape, scale):
        return jax.random.normal(k, shape, f32) * scale

    x = nrm(ks[0], (BATCH, SEQ, D_MODEL), 1.0)
    norm_mix = 1.0 + nrm(ks[1], (DEPTH, D_MODEL), 0.02)
    w_in = nrm(ks[2], (DEPTH, D_MODEL, D_IN), D_MODEL ** -0.5)
    conv_w = nrm(ks[3], (DEPTH, CONV_K, GDN_CONV_CH), CONV_K ** -0.5)
    gdn_a_log = jnp.log(jax.random.uniform(ks[4], (DEPTH, GDN_HEADS), f32, 1.0, 16.0))
    dt = jnp.exp(jax.random.uniform(ks[5], (DEPTH, GDN_HEADS), f32, math.log(1e-3), math.log(1e-1)))
    gdn_dt_bias = dt + jnp.log(-jnp.expm1(-dt))
    gdn_norm = 1.0 + nrm(ks[6], (DEPTH, GDN_DV), 0.02)
    gla_w_gate = nrm(ks[7], (DEPTH, GLA_RANK, GLA_HEADS * GLA_DK), GLA_RANK ** -0.5)
    gla_b_gate = nrm(ks[8], (DEPTH, GLA_HEADS * GLA_DK), 0.1)
    gla_norm = 1.0 + nrm(ks[9], (DEPTH, GLA_DV), 0.02)
    sb_q_norm = 1.0 + nrm(ks[10], (DEPTH, SB_DH), 0.02)
    sb_k_norm = 1.0 + nrm(ks[11], (DEPTH, SB_DH), 0.02)
    w_branch = nrm(ks[12], (DEPTH, N_BRANCH, BRANCH_WIDTH, D_MODEL), BRANCH_WIDTH ** -0.5)
    w_out = nrm(ks[13], (DEPTH, D_MODEL, D_MODEL), D_MODEL ** -0.5)
    norm_ffn = 1.0 + nrm(ks[14], (DEPTH, D_MODEL), 0.02)
    ffn_w1 = nrm(ks[15], (N_DENSE, D_MODEL, D_FF), D_MODEL ** -0.5)
    ffn_w3 = nrm(ks[16], (N_DENSE, D_MODEL, D_FF), D_MODEL ** -0.5)
    ffn_w2 = nrm(ks[17], (N_DENSE, D_FF, D_MODEL), D_FF ** -0.5)
    moe_router = nrm(ks[18], (N_MOE, D_MODEL, N_EXPERTS), D_MODEL ** -0.5)
    moe_w1 = nrm(ks[19], (N_MOE, N_EXPERTS, D_MODEL, D_FF), D_MODEL ** -0.5)
    moe_w3 = nrm(ks[20], (N_MOE, N_EXPERTS, D_MODEL, D_FF), D_MODEL ** -0.5)
    moe_w2 = nrm(ks[21], (N_MOE, N_EXPERTS, D_FF, D_MODEL), D_FF ** -0.5)
    return {'x': x, 'norm_mix': norm_mix, 'w_in': w_in, 'conv_w': conv_w, 'gdn_a_log': gdn_a_log,
            'gdn_dt_bias': gdn_dt_bias, 'gdn_norm': gdn_norm, 'gla_w_gate': gla_w_gate,
            'gla_b_gate': gla_b_gate, 'gla_norm': gla_norm, 'sb_q_norm': sb_q_norm,
            'sb_k_norm': sb_k_norm, 'w_branch': w_branch, 'w_out': w_out, 'norm_ffn': norm_ffn,
            'ffn_w1': ffn_w1, 'ffn_w3': ffn_w3, 'ffn_w2': ffn_w2, 'moe_router': moe_router,
            'moe_w1': moe_w1, 'moe_w3': moe_w3, 'moe_w2': moe_w2}


def reference(x, norm_mix, w_in, conv_w, gdn_a_log, gdn_dt_bias, gdn_norm, gla_w_gate, gla_b_gate,
              gla_norm, sb_q_norm, sb_k_norm, w_branch, w_out, norm_ffn, ffn_w1, ffn_w3, ffn_w2,
              moe_router, moe_w1, moe_w3, moe_w2):
    for layer in range(DEPTH):
        xn = rms_norm(x, norm_mix[layer])
        x = x + token_mixers(xn, w_in[layer], conv_w[layer], gdn_a_log[layer], gdn_dt_bias[layer],
                             gdn_norm[layer], gla_w_gate[layer], gla_b_gate[layer], gla_norm[layer],
                             sb_q_norm[layer], sb_k_norm[layer], w_branch[layer], w_out[layer])
        xn = rms_norm(x, norm_ffn[layer])
        i = layer // 2
        if layer % 2 == 0:
            x = x + swiglu(xn, ffn_w1[i], ffn_w3[i], ffn_w2[i])
        else:
            x = x + moe_swiglu(xn, moe_router[i], moe_w1[i], moe_w3[i], moe_w2[i])
    return x
```

```python
import functools

import jax
import jax.numpy as jnp
from jax import lax
from jax.experimental import pallas as pl
from jax.experimental.pallas import tpu as pltpu

F32 = jnp.float32
BF16 = jnp.bfloat16

D_MODEL = 2048
GDN_HEADS, GDN_DK, GDN_DV, CONV_K, CHUNK = 8, 128, 128, 4, 64
GLA_HEADS, GLA_DK, GLA_DV, GLA_RANK, GLA_NORMALIZER = 4, 128, 256, 16, 16.0
SB_HEADS, SB_DH, SB_BLOCK = 8, 128, 128
N_BRANCH, BRANCH_WIDTH = 3, 1024
D_FF, N_EXPERTS, TOP_K = 7168, 8, 2
EPS = 1e-6

LANES = 128
SUBLANES = 8
VMEM_LIMIT = 48 * 1024 * 1024

COL_GDN_Q, COL_GDN_K, COL_GDN_V, COL_GDN_Z = 0, 8, 16, 24
COL_GLA_Q, COL_GLA_K, COL_GLA_V, COL_GLA_R = 32, 36, 40, 48
COL_SB_Q, COL_SB_K, COL_SB_V = 56, 64, 72
COL_GATES = 80
N_MAIN = 128 * LANES
SM_A, SM_B, SM_G = 0, 8, 16


def _cparams(sem):
    return pltpu.CompilerParams(dimension_semantics=sem, vmem_limit_bytes=VMEM_LIMIT)


def _dot(a, b):
    return jnp.dot(a, b, preferred_element_type=F32)


def _dot_nt(a, b):
    return lax.dot_general(a, b, (((1,), (1,)), ((), ())), preferred_element_type=F32)


def _dot_tn(a, b):
    return lax.dot_general(a, b, (((0,), (0,)), ((), ())), preferred_element_type=F32)


def _sigmoid(x):
    return 1.0 / (1.0 + jnp.exp(-x))


def _silu(x):
    return x * _sigmoid(x)


def _softplus(x):
    return jnp.maximum(x, 0.0) + jnp.log1p(jnp.exp(-jnp.abs(x)))


def _log_sigmoid(x):
    return -_softplus(-x)


def _seg_cumsum_rows(x, seg):
    pos = lax.broadcasted_iota(jnp.int32, x.shape, 0) & (seg - 1)
    s = 1
    while s < seg:
        x = x + jnp.where(pos >= s, pltpu.roll(x, s, axis=0), 0.0)
        s *= 2
    return x


def _in_proj_kernel(x_ref, g_ref, wm_ref, ws_ref, proj_ref, small_ref, xn_ref):
    @pl.when(pl.program_id(1) == 0)
    def _():
        x = x_ref[...]
        ms = jnp.mean(x * x, axis=-1, keepdims=True)
        xn = (x * lax.rsqrt(ms + EPS) * g_ref[...]).astype(BF16)
        xn_ref[...] = xn
        small_ref[...] = _dot(xn, ws_ref[...])

    proj_ref[...] = _dot(xn_ref[...], wm_ref[...])


def _in_proj(x2d, gain, w_main, w_small, *, tm=512, tn=1024):
    t, d = x2d.shape
    tm = min(tm, t)
    n = w_main.shape[1]
    return pl.pallas_call(
        _in_proj_kernel,
        grid=(t // tm, n // tn),
        in_specs=[
            pl.BlockSpec((tm, d), lambda i, j: (i, 0)),
            pl.BlockSpec((1, d), lambda i, j: (0, 0)),
            pl.BlockSpec((d, tn), lambda i, j: (0, j)),
            pl.BlockSpec((d, LANES), lambda i, j: (0, 0)),
        ],
        out_specs=[
            pl.BlockSpec((tm, tn), lambda i, j: (i, j)),
            pl.BlockSpec((tm, LANES), lambda i, j: (i, 0)),
        ],
        out_shape=[jax.ShapeDtypeStruct((t, n), F32), jax.ShapeDtypeStruct((t, LANES), F32)],
        scratch_shapes=[pltpu.VMEM((tm, d), BF16)],
        compiler_params=_cparams(("parallel", "arbitrary")),
        name="in_proj",
    )(x2d, gain.reshape(1, d), w_main, w_small)


GDN_SBLK = 256


def _causal_conv_silu(x, prev, w):
    n = x.shape[0]
    row8 = lax.broadcasted_iota(jnp.int32, (SUBLANES, LANES), 0)
    acc = x * w[CONV_K - 1:CONV_K, :]
    for j in range(1, CONV_K):
        xs = pltpu.roll(x, j, axis=0)
        ps = pltpu.roll(prev, j, axis=0)
        top = jnp.where(row8 < j, ps, xs[0:SUBLANES, :])
        xs = jnp.concatenate([top, xs[SUBLANES:n, :]], axis=0)
        acc = acc + xs * w[CONV_K - 1 - j:CONV_K - j, :]
    return _silu(acc)


def _unit_lower_inverse(l_mat, row, col):
    eye = (row == col).astype(F32)
    blk16 = (row >> 4) == (col >> 4)
    blk32 = (row >> 5) == (col >> 5)
    d = jnp.where(blk16, l_mat, 0.0)
    e1 = jnp.where(blk32 & (~blk16), l_mat, 0.0)
    e2 = jnp.where(blk32, 0.0, l_mat)

    def mm(a, b):
        return _dot(a.astype(BF16), b.astype(BF16))

    d2 = mm(d, d)
    d4 = mm(d2, d2)
    d8 = mm(d4, d4)
    x = eye - d
    x = x + mm(x, d2)
    x = x + mm(x, d4)
    x = x + mm(x, d8)
    x = x - mm(mm(x, e1), x)
    x = x - mm(mm(x, e2), x)
    return x


def _gdn_kernel(q_ref, k_ref, v_ref, z_ref, sm_ref, wq_ref, wk_ref, wv_ref, alog_ref, dtb_ref,
                gn_ref, o_ref, state_ref, tail_ref, u_ref, w_ref, qk_ref, qd_ref, kd_ref,
                gl_ref, vn_ref):
    h = pl.program_id(1)
    n = GDN_SBLK
    pair = 2 * CHUNK

    @pl.when(pl.program_id(2) == 0)
    def _():
        state_ref[...] = jnp.zeros_like(state_ref)
        tail_ref[...] = jnp.zeros_like(tail_ref)

    def conv(x_ref, w_ref_, idx):
        x = x_ref[...]
        y = _causal_conv_silu(x, tail_ref[idx], w_ref_[...])
        tail_ref[idx] = x[n - SUBLANES:n, :]
        return y

    q = conv(q_ref, wq_ref, 0)
    k = conv(k_ref, wk_ref, 1)
    v = conv(v_ref, wv_ref, 2)
    q = q * lax.rsqrt(jnp.sum(q * q, axis=-1, keepdims=True) + EPS) * (GDN_DK ** -0.5)
    k = k * lax.rsqrt(jnp.sum(k * k, axis=-1, keepdims=True) + EPS)

    sm = sm_ref[...]
    lane = lax.broadcasted_iota(jnp.int32, sm.shape, 1)
    a_col = jnp.sum(jnp.where(lane == SM_A + h, sm, 0.0), axis=-1, keepdims=True)
    b_col = jnp.sum(jnp.where(lane == SM_B + h, sm, 0.0), axis=-1, keepdims=True)
    log_decay = -jnp.exp(alog_ref[...]) * _softplus(a_col + dtb_ref[...])
    beta = _sigmoid(b_col)
    g = _seg_cumsum_rows(log_decay, CHUNK)
    eg = jnp.exp(g)
    kb = k * beta
    vb = v * beta

    row = lax.broadcasted_iota(jnp.int32, (pair, pair), 0)
    col = lax.broadcasted_iota(jnp.int32, (pair, pair), 1)
    same = (row >> 6) == (col >> 6)
    causal = same & (row >= col)
    strict = same & (row > col)
    first = lax.broadcasted_iota(jnp.int32, (pair, LANES), 0) < CHUNK

    for p in range(n // pair):
        sl = slice(p * pair, (p + 1) * pair)
        g_p = g[sl, :]
        g_rowmat = g_p.T
        decay = jnp.exp(jnp.where(causal, g_p - g_rowmat, -jnp.inf))
        k_p = k[sl, :].astype(BF16)
        kk = _dot_nt(kb[sl, :].astype(BF16), k_p)
        l_mat = jnp.where(strict, kk * decay, 0.0)
        x_inv = _unit_lower_inverse(l_mat, row, col).astype(BF16)
        u_ref[sl, :] = _dot(x_inv, vb[sl, :].astype(BF16))
        w_ref[sl, :] = _dot(x_inv, (kb[sl, :] * eg[sl, :]).astype(BF16)).astype(BF16)
        qk = _dot_nt(q[sl, :].astype(BF16), k_p) * decay
        qk_ref[sl, :] = jnp.where(causal, qk, 0.0).astype(BF16)
        qd_ref[sl, :] = (q[sl, :] * eg[sl, :]).astype(BF16)
        g_last = jnp.where(first, g_p[CHUNK - 1:CHUNK, :], g_p[pair - 1:pair, :])
        kd_ref[sl, :] = (k[sl, :] * jnp.exp(g_last - g_p)).astype(BF16)
        gl_ref[sl, :] = jnp.exp(g_last)

    for c in range(n // CHUNK):
        sl = slice(c * CHUNK, (c + 1) * CHUNK)
        half = c % 2
        if half == 0:
            vn_ref[...] = jnp.zeros_like(vn_ref)
        state = state_ref[...]
        sb = state.astype(BF16)
        v_new = u_ref[sl, :] - _dot(w_ref[sl, :], sb)
        vnb = v_new.astype(BF16)
        vn_ref[half * CHUNK:(half + 1) * CHUNK, :] = vnb
        o = _dot(qd_ref[sl, :], sb) + _dot(qk_ref[sl, :], vn_ref[...])
        gl = gl_ref[c * CHUNK:c * CHUNK + 1, :]
        state_ref[...] = state * gl + _dot_tn(kd_ref[sl, :], vnb)
        ms = jnp.mean(o * o, axis=-1, keepdims=True)
        o = o * lax.rsqrt(ms + EPS) * gn_ref[...]
        o_ref[sl, :] = (o * _silu(z_ref[sl, :])).astype(o_ref.dtype)


def _gdn(proj, small, conv_w, a_log, dt_bias, gnorm, batch, seq):
    t = proj.shape[0]
    n = GDN_SBLK
    ns = seq // n
    cw = conv_w.reshape(CONV_K, 3 * GDN_HEADS, LANES).transpose(1, 0, 2)
    alog = jnp.broadcast_to(a_log.astype(F32)[:, None, None], (GDN_HEADS, 1, LANES))
    dtb = jnp.broadcast_to(dt_bias.astype(F32)[:, None, None], (GDN_HEADS, 1, LANES))

    def col(off):
        return pl.BlockSpec((n, LANES), lambda b, h, s: (b * ns + s, off + h))

    def cws(off):
        return pl.BlockSpec((None, CONV_K, LANES), lambda b, h, s: (off + h, 0, 0))

    par = pl.BlockSpec((None, 1, LANES), lambda b, h, s: (h, 0, 0))
    return pl.pallas_call(
        _gdn_kernel,
        grid=(batch, GDN_HEADS, ns),
        in_specs=[col(COL_GDN_Q), col(COL_GDN_K), col(COL_GDN_V), col(COL_GDN_Z),
                  pl.BlockSpec((n, LANES), lambda b, h, s: (b * ns + s, 0)),
                  cws(0), cws(GDN_HEADS), cws(2 * GDN_HEADS), par, par,
                  pl.BlockSpec((1, LANES), lambda b, h, s: (0, 0))],
        out_specs=pl.BlockSpec((n, LANES), lambda b, h, s: (b * ns + s, h)),
        out_shape=jax.ShapeDtypeStruct((t, BRANCH_WIDTH), BF16),
        scratch_shapes=[
            pltpu.VMEM((GDN_DK, GDN_DV), F32),
            pltpu.VMEM((3, SUBLANES, LANES), F32),
            pltpu.VMEM((n, LANES), F32),
            pltpu.VMEM((n, LANES), BF16),
            pltpu.VMEM((n, LANES), BF16),
            pltpu.VMEM((n, LANES), BF16),
            pltpu.VMEM((n, LANES), BF16),
            pltpu.VMEM((n, LANES), F32),
            pltpu.VMEM((2 * CHUNK, LANES), BF16),
        ],
        compiler_params=_cparams(("parallel", "parallel", "arbitrary")),
        name="gdn",
    )(proj, proj, proj, proj, small, cw, cw, cw, alog, dtb, gnorm.reshape(1, LANES))


GLA_SBLK = 256


def _gla_kernel(q_ref, k_ref, v_ref, r_ref, sm_ref, wg_ref, bg_ref, gn_ref, o_ref, state_ref):
    n = GLA_SBLK

    @pl.when(pl.program_id(2) == 0)
    def _():
        state_ref[...] = jnp.zeros_like(state_ref)

    pre = _dot(sm_ref[...].astype(BF16), wg_ref[...]) + bg_ref[...]
    log_g = _log_sigmoid(pre) / GLA_NORMALIZER
    gcum = _seg_cumsum_rows(log_g, CHUNK)
    gcum_t = gcum.T
    q = q_ref[...] * (GLA_DK ** -0.5)
    k = k_ref[...]
    row = lax.broadcasted_iota(jnp.int32, (CHUNK, CHUNK), 0)
    col = lax.broadcasted_iota(jnp.int32, (CHUNK, CHUNK), 1)
    causal = row >= col

    for c in range(n // CHUNK):
        sl = slice(c * CHUNK, (c + 1) * CHUNK)
        g_c = gcum[sl, :]
        ref = g_c[CHUNK // 2 - 1:CHUNK // 2, :]
        g_end = g_c[CHUNK - 1:CHUNK, :]
        q_c, k_c = q[sl, :], k[sl, :]
        v_c = v_ref[sl, :].astype(BF16)
        a = _dot_nt((q_c * jnp.exp(g_c - ref)).astype(BF16), (k_c * jnp.exp(ref - g_c)).astype(BF16))
        a = jnp.where(causal, a, 0.0).astype(BF16)
        state = state_ref[...]
        o = _dot((q_c * jnp.exp(g_c)).astype(BF16), state.astype(BF16)) + _dot(a, v_c)
        k_dec = (k_c * jnp.exp(g_end - g_c)).astype(BF16)
        gl_col = jnp.exp(gcum_t[:, (c + 1) * CHUNK - 1:(c + 1) * CHUNK])
        state_ref[...] = state * gl_col + _dot_tn(k_dec, v_c)
        ms = jnp.mean(o * o, axis=-1, keepdims=True)
        o = o * lax.rsqrt(ms + EPS) * gn_ref[...]
        o_ref[sl, :] = (o * _silu(r_ref[sl, :])).astype(o_ref.dtype)


def _gla(proj, small, w_gate, b_gate, gnorm, batch, seq):
    t = proj.shape[0]
    n = GLA_SBLK
    ns = seq // n
    wg = jnp.zeros((LANES, GLA_HEADS * GLA_DK), BF16).at[SM_G:SM_G + GLA_RANK, :].set(w_gate.astype(BF16))
    rows = lambda b, h, s: b * ns + s
    return pl.pallas_call(
        _gla_kernel,
        grid=(batch, GLA_HEADS, ns),
        in_specs=[
            pl.BlockSpec((n, GLA_DK), lambda b, h, s: (rows(b, h, s), COL_GLA_Q + h)),
            pl.BlockSpec((n, GLA_DK), lambda b, h, s: (rows(b, h, s), COL_GLA_K + h)),
            pl.BlockSpec((n, GLA_DV), lambda b, h, s: (rows(b, h, s), COL_GLA_V // 2 + h)),
            pl.BlockSpec((n, GLA_DV), lambda b, h, s: (rows(b, h, s), COL_GLA_R // 2 + h)),
            pl.BlockSpec((n, LANES), lambda b, h, s: (rows(b, h, s), 0)),
            pl.BlockSpec((LANES, GLA_DK), lambda b, h, s: (0, h)),
            pl.BlockSpec((1, GLA_DK), lambda b, h, s: (0, h)),
            pl.BlockSpec((1, GLA_DV), lambda b, h, s: (0, 0)),
        ],
        out_specs=pl.BlockSpec((n, GLA_DV), lambda b, h, s: (rows(b, h, s), h)),
        out_shape=jax.ShapeDtypeStruct((t, BRANCH_WIDTH), BF16),
        scratch_shapes=[pltpu.VMEM((GLA_DK, GLA_DV), F32)],
        compiler_params=_cparams(("parallel", "parallel", "arbitrary")),
        name="gla",
    )(proj, proj, proj, proj, small, wg, b_gate.reshape(1, -1).astype(F32), gnorm.reshape(1, GLA_DV))


def _split_hi_lo(x):
    hi = x.astype(BF16)
    lo = (x - hi.astype(F32)).astype(BF16)
    return hi, lo


def _sb_kernel(q_ref, k_ref, v_ref, qg_ref, kg_ref, o_ref, kn_ref):
    qb = pl.program_id(2)
    blk = SB_BLOCK

    @pl.when(qb == 0)
    def _():
        kf = k_ref[...]
        ms = jnp.mean(kf * kf, axis=-1, keepdims=True)
        kn_ref[...] = (kf * lax.rsqrt(ms + EPS) * kg_ref[...]).astype(BF16)

    q = q_ref[...]
    ms = jnp.mean(q * q, axis=-1, keepdims=True)
    qn = (q * lax.rsqrt(ms + EPS) * qg_ref[...]).astype(BF16)
    row = lax.broadcasted_iota(jnp.int32, (blk, blk), 0)
    col = lax.broadcasted_iota(jnp.int32, (blk, blk), 1)
    strict = col < row
    after = (row > col).astype(BF16)
    scale = SB_DH ** -0.5

    def block(j, carry, acc, diag):
        off = pl.multiple_of(j * blk, blk)
        z = _dot_nt(qn, kn_ref[pl.ds(off, blk), :]) * scale
        log_beta = _log_sigmoid(z)
        log_1m = log_beta - z
        if diag:
            log_1m = jnp.where(strict, log_1m, 0.0)
        hi, lo = _split_hi_lo(log_1m)
        tail = _dot(hi, after) + _dot(lo, after) + carry
        attn = jnp.exp(log_beta + tail)
        if diag:
            attn = jnp.where(strict, attn, 0.0)
        acc = acc + _dot(attn.astype(BF16), v_ref[pl.ds(off, blk), :].astype(BF16))
        carry = carry + jnp.sum(log_1m, axis=-1, keepdims=True)
        return carry, acc

    carry, acc = block(qb, jnp.zeros((blk, 1), F32), jnp.zeros((blk, SB_DH), F32), True)

    def body(i, state):
        return block(qb - 1 - i, state[0], state[1], False)

    carry, acc = lax.fori_loop(0, qb, body, (carry, acc))
    o_ref[...] = acc.astype(o_ref.dtype)


def _sb(proj, q_gain, k_gain, batch, seq):
    t = proj.shape[0]
    nq = seq // SB_BLOCK
    return pl.pallas_call(
        _sb_kernel,
        grid=(batch, SB_HEADS, nq),
        in_specs=[
            pl.BlockSpec((SB_BLOCK, SB_DH), lambda b, h, i: (b * nq + i, COL_SB_Q + h)),
            pl.BlockSpec((seq, SB_DH), lambda b, h, i: (b, COL_SB_K + h)),
            pl.BlockSpec((seq, SB_DH), lambda b, h, i: (b, COL_SB_V + h)),
            pl.BlockSpec((1, SB_DH), lambda b, h, i: (0, 0)),
            pl.BlockSpec((1, SB_DH), lambda b, h, i: (0, 0)),
        ],
        out_specs=pl.BlockSpec((SB_BLOCK, SB_DH), lambda b, h, i: (b * nq + i, h)),
        out_shape=jax.ShapeDtypeStruct((t, BRANCH_WIDTH), BF16),
        scratch_shapes=[pltpu.VMEM((seq, SB_DH), BF16)],
        compiler_params=_cparams(("parallel", "parallel", "arbitrary")),
        name="stickbreak",
    )(proj, proj, proj, q_gain.reshape(1, SB_DH), k_gain.reshape(1, SB_DH))


def _merge_kernel(ya_ref, yb_ref, yc_ref, wb_ref, ga_ref, gb_ref, gc_ref, o_ref):
    acc = _sigmoid(ga_ref[...]) * _dot(ya_ref[...], wb_ref[0])
    acc = acc + _sigmoid(gb_ref[...]) * _dot(yb_ref[...], wb_ref[1])
    acc = acc + _sigmoid(gc_ref[...]) * _dot(yc_ref[...], wb_ref[2])
    o_ref[...] = acc.astype(o_ref.dtype)


def _merge(ya, yb, yc, proj, w_branch, *, tm=512, tn=512):
    t = ya.shape[0]
    tm = min(tm, t)
    nj = D_MODEL // tn
    g0 = COL_GATES * LANES // tn
    ysp = pl.BlockSpec((tm, BRANCH_WIDTH), lambda i, j: (i, 0))

    def gate(nb):
        return pl.BlockSpec((tm, tn), lambda i, j: (i, g0 + nb * nj + j))

    return pl.pallas_call(
        _merge_kernel,
        grid=(t // tm, nj),
        in_specs=[ysp, ysp, ysp,
                  pl.BlockSpec((N_BRANCH, BRANCH_WIDTH, tn), lambda i, j: (0, 0, j)),
                  gate(0), gate(1), gate(2)],
        out_specs=pl.BlockSpec((tm, tn), lambda i, j: (i, j)),
        out_shape=jax.ShapeDtypeStruct((t, D_MODEL), BF16),
        compiler_params=_cparams(("parallel", "arbitrary")),
        name="merge",
    )(ya, yb, yc, w_branch, proj, proj, proj)


def _out_proj_kernel(m_ref, w_ref, x_ref, g_ref, x1_ref, xn_ref):
    x1 = x_ref[...] + _dot(m_ref[...], w_ref[...])
    x1_ref[...] = x1
    ms = jnp.mean(x1 * x1, axis=-1, keepdims=True)
    xn_ref[...] = (x1 * lax.rsqrt(ms + EPS) * g_ref[...]).astype(BF16)


def _out_proj(merged, w_out, x2d, gain, *, tm=256):
    t, d = x2d.shape
    tm = min(tm, t)
    rows = pl.BlockSpec((tm, d), lambda i: (i, 0))
    return pl.pallas_call(
        _out_proj_kernel,
        grid=(t // tm,),
        in_specs=[rows, pl.BlockSpec((d, d), lambda i: (0, 0)), rows,
                  pl.BlockSpec((1, d), lambda i: (0, 0))],
        out_specs=[rows, rows],
        out_shape=[jax.ShapeDtypeStruct((t, d), F32), jax.ShapeDtypeStruct((t, d), BF16)],
        compiler_params=_cparams(("parallel",)),
        name="out_proj",
    )(merged, w_out, x2d, gain.reshape(1, d))


FFN_RB = 512
FFN_FC = 512


def _ffn_kernel(be_ref, valid_ref, x_ref, w1_ref, w3_ref, w2_ref, *rest, residual):
    if residual:
        res_ref, o_ref, acc_ref = rest
    else:
        o_ref, acc_ref = rest
    i, c = pl.program_id(0), pl.program_id(1)

    @pl.when(c == 0)
    def _():
        acc_ref[...] = jnp.zeros_like(acc_ref)

    @pl.when(valid_ref[i] == 1)
    def _():
        x = x_ref[...]
        h1 = _dot(x, w1_ref[...])
        h3 = _dot(x, w3_ref[...])
        acc_ref[...] += _dot((_silu(h1) * h3).astype(BF16), w2_ref[...])

    @pl.when(c == pl.num_programs(1) - 1)
    def _():
        y = acc_ref[...]
        if residual:
            y = y + res_ref[...]
        o_ref[...] = y.astype(o_ref.dtype)


def _ffn(x_rows, w1, w3, w2, block_expert, block_valid, residual=None, out_dtype=F32):
    r, d = x_rows.shape
    f = w1.shape[-1]
    rb = min(FFN_RB, r)
    nc = f // FFN_FC
    last = nc - 1

    def ccol(c, valid):
        return c * valid + last * (1 - valid)

    rows = pl.BlockSpec((rb, d), lambda i, c, be, va: (i, 0))
    in_specs = [
        rows,
        pl.BlockSpec((None, d, FFN_FC), lambda i, c, be, va: (be[i], 0, ccol(c, va[i]))),
        pl.BlockSpec((None, d, FFN_FC), lambda i, c, be, va: (be[i], 0, ccol(c, va[i]))),
        pl.BlockSpec((None, FFN_FC, d), lambda i, c, be, va: (be[i], ccol(c, va[i]), 0)),
    ]
    args = [x_rows, w1, w3, w2]
    if residual is not None:
        in_specs.append(rows)
        args.append(residual)
    return pl.pallas_call(
        functools.partial(_ffn_kernel, residual=residual is not None),
        grid_spec=pltpu.PrefetchScalarGridSpec(
            num_scalar_prefetch=2,
            grid=(r // rb, nc),
            in_specs=in_specs,
            out_specs=rows,
            scratch_shapes=[pltpu.VMEM((rb, d), F32)],
        ),
        out_shape=jax.ShapeDtypeStruct((r, d), out_dtype),
        compiler_params=_cparams(("parallel", "arbitrary")),
        name="swiglu",
    )(block_expert, block_valid, *args)


ROUTE_TT = 512
RT_W, RT_RANK, RT_MEMBER = 0, 8, 16


def _router_kernel(x_ref, g_ref, wr_ref, rowtab_ref, coltab_ref, cum_ref, count_ref):
    i = pl.program_id(0)
    tt = x_ref.shape[0]

    @pl.when(i == 0)
    def _():
        count_ref[...] = jnp.zeros_like(count_ref)

    x = x_ref[...]
    ms = jnp.mean(x * x, axis=-1, keepdims=True)
    xn = x * lax.rsqrt(ms + EPS) * g_ref[...]
    logits = jnp.dot(xn, wr_ref[...], preferred_element_type=F32, precision=lax.Precision.HIGHEST)
    lg = logits.T[0:N_EXPERTS, :]
    eidx = lax.broadcasted_iota(jnp.int32, lg.shape, 0).astype(F32)
    m1 = jnp.max(lg, axis=0, keepdims=True)
    i1 = jnp.min(jnp.where(lg == m1, eidx, float(N_EXPERTS)), axis=0, keepdims=True)
    sel1 = eidx == i1
    lg2 = jnp.where(sel1, -jnp.inf, lg)
    m2 = jnp.max(lg2, axis=0, keepdims=True)
    i2 = jnp.min(jnp.where(lg2 == m2, eidx, float(N_EXPERTS)), axis=0, keepdims=True)
    sel2 = eidx == i2
    e2 = jnp.exp(m2 - m1)
    den = 1.0 + e2
    wts = jnp.where(sel1, 1.0 / den, 0.0) + jnp.where(sel2, e2 / den, 0.0)
    member = (sel1 | sel2).astype(F32)

    srow = lax.broadcasted_iota(jnp.int32, (tt, tt), 0)
    scol = lax.broadcasted_iota(jnp.int32, (tt, tt), 1)
    before = (srow < scol).astype(BF16)
    base = count_ref[...]
    rank = _dot(member.astype(BF16), before) + base
    cum_ref[...] = jnp.broadcast_to(base, (N_EXPERTS, LANES))[None]
    count_ref[...] = base + jnp.sum(member, axis=1, keepdims=True)

    rowtab = jnp.concatenate([wts, rank, member], axis=0)
    rowtab_ref[...] = rowtab
    pad = jnp.zeros((LANES - 3 * N_EXPERTS, tt), F32)
    coltab_ref[...] = jnp.concatenate([rowtab, pad], axis=0).T


def _router(x1, gain, w_router):
    t, d = x1.shape
    tt = min(ROUTE_TT, t)
    nt = t // tt
    wr = jnp.zeros((d, LANES), F32).at[:, :N_EXPERTS].set(w_router.astype(F32))
    return pl.pallas_call(
        _router_kernel,
        grid=(nt,),
        in_specs=[pl.BlockSpec((tt, d), lambda i: (i, 0)),
                  pl.BlockSpec((1, d), lambda i: (0, 0)),
                  pl.BlockSpec((d, LANES), lambda i: (0, 0))],
        out_specs=[pl.BlockSpec((3 * N_EXPERTS, tt), lambda i: (0, i)),
                   pl.BlockSpec((tt, LANES), lambda i: (i, 0)),
                   pl.BlockSpec((1, N_EXPERTS, LANES), lambda i: (i, 0, 0))],
        out_shape=[jax.ShapeDtypeStruct((3 * N_EXPERTS, t), F32),
                   jax.ShapeDtypeStruct((t, LANES), F32),
                   jax.ShapeDtypeStruct((nt, N_EXPERTS, LANES), F32)],
        scratch_shapes=[pltpu.VMEM((N_EXPERTS, 1), F32)],
        compiler_params=_cparams(("arbitrary",)),
        name="router",
    )(x1, gain.reshape(1, d), wr)


def _gather_kernel(pb_ref, pt_ref, pe_ref, plo_ref, pfirst_ref, pskip_ref, x_ref, rt_ref, o_ref):
    g = pl.program_id(0)
    rb, tt = o_ref.shape[0], x_ref.shape[0]

    @pl.when(pfirst_ref[g] == 1)
    def _():
        o_ref[...] = jnp.zeros_like(o_ref)

    @pl.when(pskip_ref[g] == 0)
    def _():
        e = pe_ref[g]
        rt = rt_ref[...]
        ridx = lax.broadcasted_iota(jnp.int32, rt.shape, 0)
        rank = jnp.sum(jnp.where(ridx == RT_RANK + e, rt, 0.0), axis=0, keepdims=True)
        member = jnp.sum(jnp.where(ridx == RT_MEMBER + e, rt, 0.0), axis=0, keepdims=True)
        target = jnp.where(member > 0.5, rank - plo_ref[g].astype(F32), -1.0)
        rows = lax.broadcasted_iota(jnp.int32, (rb, tt), 0).astype(F32)
        onehot = (rows == target).astype(BF16)
        o_ref[...] += _dot(onehot, x_ref[...]).astype(o_ref.dtype)


def _gather_rows(xn, rowtab, pairs, nb):
    t, d = xn.shape
    tt = min(ROUTE_TT, t)
    rb = FFN_RB
    npairs = pairs[0].shape[0]
    return pl.pallas_call(
        _gather_kernel,
        grid_spec=pltpu.PrefetchScalarGridSpec(
            num_scalar_prefetch=6,
            grid=(npairs,),
            in_specs=[pl.BlockSpec((tt, d), lambda g, pb, pt, *_: (pt[g], 0)),
                      pl.BlockSpec((3 * N_EXPERTS, tt), lambda g, pb, pt, *_: (0, pt[g]))],
            out_specs=pl.BlockSpec((rb, d), lambda g, pb, *_: (pb[g], 0)),
        ),
        out_shape=jax.ShapeDtypeStruct((nb * rb, d), BF16),
        compiler_params=_cparams(("arbitrary",)),
        name="moe_gather",
    )(*pairs, xn, rowtab)


def _combine_kernel(pb_ref, pt_ref, pe_ref, plo_ref, pfirst_ref, pskip_ref, y_ref, ct_ref, x_ref,
                    o_ref):
    g = pl.program_id(0)
    tt, rb = o_ref.shape[0], y_ref.shape[0]

    @pl.when(pfirst_ref[g] == 1)
    def _():
        o_ref[...] = x_ref[...]

    @pl.when(pskip_ref[g] == 0)
    def _():
        e = pe_ref[g]
        ct = ct_ref[...]
        lane = lax.broadcasted_iota(jnp.int32, ct.shape, 1)
        wcol = jnp.sum(jnp.where(lane == RT_W + e, ct, 0.0), axis=1, keepdims=True)
        rank = jnp.sum(jnp.where(lane == RT_RANK + e, ct, 0.0), axis=1, keepdims=True)
        member = jnp.sum(jnp.where(lane == RT_MEMBER + e, ct, 0.0), axis=1, keepdims=True)
        target = jnp.where(member > 0.5, rank - plo_ref[g].astype(F32), -1.0)
        cols = lax.broadcasted_iota(jnp.int32, (tt, rb), 1).astype(F32)
        onehot = (cols == target).astype(BF16)
        o_ref[...] += wcol * _dot(onehot, y_ref[...])


def _combine(y_rows, coltab, x1, pairs):
    t, d = x1.shape
    tt = min(ROUTE_TT, t)
    rb = FFN_RB
    npairs = pairs[0].shape[0]
    return pl.pallas_call(
        _combine_kernel,
        grid_spec=pltpu.PrefetchScalarGridSpec(
            num_scalar_prefetch=6,
            grid=(npairs,),
            in_specs=[pl.BlockSpec((rb, d), lambda g, pb, *_: (pb[g], 0)),
                      pl.BlockSpec((tt, LANES), lambda g, pb, pt, *_: (pt[g], 0)),
                      pl.BlockSpec((tt, d), lambda g, pb, pt, *_: (pt[g], 0))],
            out_specs=pl.BlockSpec((tt, d), lambda g, pb, pt, *_: (pt[g], 0)),
        ),
        out_shape=jax.ShapeDtypeStruct((t, d), F32),
        compiler_params=_cparams(("arbitrary",)),
        name="moe_combine",
    )(*pairs, y_rows, coltab, x1)


def _pair_tables(cum, counts, nb, order):
    nt = cum.shape[0]
    rb = FFN_RB
    npairs = nb + N_EXPERTS * nt
    padded = (counts + rb - 1) // rb * rb
    seg_end = jnp.cumsum(padded)
    seg_start = seg_end - padded
    blk = jnp.arange(nb, dtype=jnp.int32)
    blk_row = blk * rb
    blk_e = jnp.minimum(jnp.searchsorted(seg_end, blk_row, side='right'), N_EXPERTS - 1).astype(jnp.int32)
    blk_lo = blk_row - seg_start[blk_e]
    blk_valid = (blk_row < seg_end[-1]) & (blk_lo < counts[blk_e])
    blk_hi = jnp.minimum(blk_lo + rb, counts[blk_e])
    tile_lo = cum[:, blk_e]
    tile_hi = jnp.concatenate([cum[1:], counts[None, :]], axis=0)[:, blk_e]
    overlap = blk_valid[None, :] & (tile_lo < blk_hi[None, :]) & (tile_hi > blk_lo[None, :]) \
        & (tile_hi > tile_lo)
    if order == 'block':
        flat = overlap.T.reshape(-1)
        idx = jnp.nonzero(flat, size=npairs, fill_value=-1)[0].astype(jnp.int32)
        n_valid = jnp.sum(flat.astype(jnp.int32))
        last = idx[jnp.maximum(n_valid - 1, 0)]
        idx = jnp.where(idx < 0, last, idx)
        pb, pt = idx // nt, idx % nt
        major = pb
    else:
        flat = overlap.reshape(-1)
        idx = jnp.nonzero(flat, size=npairs, fill_value=-1)[0].astype(jnp.int32)
        n_valid = jnp.sum(flat.astype(jnp.int32))
        last = idx[jnp.maximum(n_valid - 1, 0)]
        idx = jnp.where(idx < 0, last, idx)
        pt, pb = idx // nb, idx % nb
        major = pt
    pos = jnp.arange(npairs, dtype=jnp.int32)
    skip = (pos >= n_valid).astype(jnp.int32)
    first = jnp.concatenate([jnp.ones((1,), jnp.int32),
                             (major[1:] != major[:-1]).astype(jnp.int32)]) * (1 - skip)
    pairs = (pb.astype(jnp.int32), pt.astype(jnp.int32), blk_e[pb], blk_lo[pb].astype(jnp.int32),
             first.astype(jnp.int32), skip)
    return pairs, blk_e, blk_valid.astype(jnp.int32)


def _moe(x1, xn, gain, w_router, w1, w3, w2):
    t, d = x1.shape
    rowtab, coltab, cum3 = _router(x1, gain, w_router)
    cum = cum3[:, :, 0].astype(jnp.int32)
    tt = min(ROUTE_TT, t)
    last_members = jnp.sum(rowtab[RT_MEMBER:RT_MEMBER + N_EXPERTS, t - tt:], axis=1).astype(jnp.int32)
    counts = cum[-1] + last_members
    nb = (t * TOP_K) // FFN_RB + N_EXPERTS
    pairs_g, blk_e, blk_valid = _pair_tables(cum, counts, nb, 'block')
    pairs_c, _, _ = _pair_tables(cum, counts, nb, 'tile')
    x_rows = _gather_rows(xn, rowtab, pairs_g, nb)
    y_rows = _ffn(x_rows, w1, w3, w2, blk_e, blk_valid, out_dtype=BF16)
    return _combine(y_rows, coltab, x1, pairs_c)


def _split_w_in(w):
    w = w.astype(BF16)
    main = jnp.concatenate([w[:, 0:4096], w[:, 4112:7184], w[:, 7200:16416]], axis=1)
    small = jnp.concatenate([w[:, 4096:4112], w[:, 7184:7200],
                             jnp.zeros((w.shape[0], LANES - 32), BF16)], axis=1)
    return main, small


def kernel(x, norm_mix, w_in, conv_w, gdn_a_log, gdn_dt_bias, gdn_norm, gla_w_gate, gla_b_gate,
           gla_norm, sb_q_norm, sb_k_norm, w_branch, w_out, norm_ffn, ffn_w1, ffn_w3, ffn_w2,
           moe_router, moe_w1, moe_w3, moe_w2):
    batch, seq, d = x.shape
    t = batch * seq
    depth = w_in.shape[0]
    x2 = x.reshape(t, d)
    for layer in range(depth):
        w_main, w_small = _split_w_in(w_in[layer])
        proj, small = _in_proj(x2, norm_mix[layer], w_main, w_small)
        ya = _gdn(proj, small, conv_w[layer], gdn_a_log[layer], gdn_dt_bias[layer], gdn_norm[layer],
                  batch, seq)
        yb = _gla(proj, small, gla_w_gate[layer], gla_b_gate[layer], gla_norm[layer], batch, seq)
        yc = _sb(proj, sb_q_norm[layer], sb_k_norm[layer], batch, seq)
        merged = _merge(ya, yb, yc, proj, w_branch[layer].astype(BF16))
        x1, xn = _out_proj(merged, w_out[layer].astype(BF16), x2, norm_ffn[layer])
        i = layer // 2
        if layer % 2 == 0:
            nblk = t // min(FFN_RB, t)
            x2 = _ffn(xn, ffn_w1[i][None].astype(BF16), ffn_w3[i][None].astype(BF16),
                      ffn_w2[i][None].astype(BF16), jnp.zeros((nblk,), jnp.int32),
                      jnp.ones((nblk,), jnp.int32), residual=x1)
        else:
            x2 = _moe(x1, xn, norm_ffn[layer], moe_router[i], moe_w1[i].astype(BF16),
                      moe_w3[i].astype(BF16), moe_w2[i].astype(BF16))
    return x2.reshape(batch, seq, d)
```

```python
import functools

import jax
import jax.numpy as jnp
from jax import lax
from jax.experimental import pallas as pl
from jax.experimental.pallas import tpu as pltpu

F32 = jnp.float32
BF16 = jnp.bfloat16

D_MODEL = 2048
GDN_HEADS, GDN_DK, GDN_DV, CONV_K, CHUNK = 8, 128, 128, 4, 64
GLA_HEADS, GLA_DK, GLA_DV, GLA_RANK, GLA_NORMALIZER = 4, 128, 256, 16, 16.0
SB_HEADS, SB_DH, SB_BLOCK = 8, 128, 128
N_BRANCH, BRANCH_WIDTH = 3, 1024
D_FF, N_EXPERTS, TOP_K = 7168, 8, 2
EPS = 1e-6

LANES = 128
SUBLANES = 8
VMEM_LIMIT = 48 * 1024 * 1024

COL_GDN_Q, COL_GDN_K, COL_GDN_V, COL_GDN_Z = 0, 8, 16, 24
COL_GLA_Q, COL_GLA_K, COL_GLA_V, COL_GLA_R = 32, 36, 40, 48
COL_SB_Q, COL_SB_K, COL_SB_V = 56, 64, 72
COL_GATES = 80
N_MAIN = 128 * LANES
SM_A, SM_B, SM_G = 0, 8, 16


def _cparams(sem):
    return pltpu.CompilerParams(dimension_semantics=sem, vmem_limit_bytes=VMEM_LIMIT)


def _dot(a, b):
    return jnp.dot(a, b, preferred_element_type=F32)


def _dot_nt(a, b):
    return lax.dot_general(a, b, (((1,), (1,)), ((), ())), preferred_element_type=F32)


def _dot_tn(a, b):
    return lax.dot_general(a, b, (((0,), (0,)), ((), ())), preferred_element_type=F32)


def _sigmoid(x):
    return 1.0 / (1.0 + jnp.exp(-x))


def _silu(x):
    return x * _sigmoid(x)


def _softplus(x):
    return jnp.maximum(x, 0.0) + jnp.log1p(jnp.exp(-jnp.abs(x)))


def _log_sigmoid(x):
    return -_softplus(-x)


def _seg_cumsum_rows(x, seg):
    pos = lax.broadcasted_iota(jnp.int32, x.shape, 0) & (seg - 1)
    s = 1
    while s < seg:
        x = x + jnp.where(pos >= s, pltpu.roll(x, s, axis=0), 0.0)
        s *= 2
    return x


def _in_proj_kernel(x_ref, g_ref, wm_ref, ws_ref, proj_ref, small_ref, xn_ref):
    @pl.when(pl.program_id(1) == 0)
    def _():
        x = x_ref[...]
        ms = jnp.mean(x * x, axis=-1, keepdims=True)
        xn = (x * lax.rsqrt(ms + EPS) * g_ref[...]).astype(BF16)
        xn_ref[...] = xn
        small_ref[...] = _dot(xn, ws_ref[...])

    proj_ref[...] = _dot(xn_ref[...], wm_ref[...])


def _in_proj(x2d, gain, w_main, w_small, *, tm=1024, tn=1024):
    t, d = x2d.shape
    tm = min(tm, t)
    n = w_main.shape[1]
    return pl.pallas_call(
        _in_proj_kernel,
        grid=(t // tm, n // tn),
        in_specs=[
            pl.BlockSpec((tm, d), lambda i, j: (i, 0)),
            pl.BlockSpec((1, d), lambda i, j: (0, 0)),
            pl.BlockSpec((d, tn), lambda i, j: (0, j)),
            pl.BlockSpec((d, LANES), lambda i, j: (0, 0)),
        ],
        out_specs=[
            pl.BlockSpec((tm, tn), lambda i, j: (i, j)),
            pl.BlockSpec((tm, LANES), lambda i, j: (i, 0)),
        ],
        out_shape=[jax.ShapeDtypeStruct((t, n), F32), jax.ShapeDtypeStruct((t, LANES), F32)],
        scratch_shapes=[pltpu.VMEM((tm, d), BF16)],
        compiler_params=_cparams(("parallel", "arbitrary")),
        name="in_proj",
    )(x2d, gain.reshape(1, d), w_main, w_small)


GDN_SBLK = 256


def _causal_conv_silu(x, prev, w):
    n = x.shape[0]
    row8 = lax.broadcasted_iota(jnp.int32, (SUBLANES, LANES), 0)
    acc = x * w[CONV_K - 1:CONV_K, :]
    for j in range(1, CONV_K):
        xs = pltpu.roll(x, j, axis=0)
        ps = pltpu.roll(prev, j, axis=0)
        top = jnp.where(row8 < j, ps, xs[0:SUBLANES, :])
        xs = jnp.concatenate([top, xs[SUBLANES:n, :]], axis=0)
        acc = acc + xs * w[CONV_K - 1 - j:CONV_K - j, :]
    return _silu(acc)


def _unit_lower_inverses(l_mats, row, col):
    eye = (row == col).astype(F32)
    blk16 = (row >> 4) == (col >> 4)
    blk32 = (row >> 5) == (col >> 5)
    bf = lambda ms: [m.astype(BF16) for m in ms]
    mm = lambda xs, ys: [_dot(a, b) for a, b in zip(xs, ys)]

    d32 = [jnp.where(blk16, l, 0.0) for l in l_mats]
    d = bf(d32)
    e1 = bf([jnp.where(blk32 & (~blk16), l, 0.0) for l in l_mats])
    e2 = bf([jnp.where(blk32, 0.0, l) for l in l_mats])
    d2 = bf(mm(d, d))
    d4 = bf(mm(d2, d2))
    d8 = bf(mm(d4, d4))
    x = [eye - m for m in d32]
    for power in (d2, d4, d8):
        x = [a + b for a, b in zip(x, mm(bf(x), power))]
    for e in (e1, e2):
        xb = bf(x)
        x = [a - b for a, b in zip(x, mm(bf(mm(xb, e)), xb))]
    return x


GDN_HG = 4


def _gdn_kernel(q_ref, k_ref, v_ref, z_ref, sm_ref, wq_ref, wk_ref, wv_ref, alog_ref, dtb_ref,
                gn_ref, o_ref, state_ref, tail_ref, rhs_ref, u_ref, w_ref, qk_ref, qd_ref, kd_ref,
                gl_ref, vn_ref):
    hg = pl.program_id(1)
    n = GDN_SBLK
    pair = 2 * CHUNK

    @pl.when(pl.program_id(2) == 0)
    def _():
        state_ref[...] = jnp.zeros_like(state_ref)
        tail_ref[...] = jnp.zeros_like(tail_ref)

    sm = sm_ref[...]
    lane = lax.broadcasted_iota(jnp.int32, sm.shape, 1)
    row = lax.broadcasted_iota(jnp.int32, (pair, pair), 0)
    col = lax.broadcasted_iota(jnp.int32, (pair, pair), 1)
    same = (row >> 6) == (col >> 6)
    causal = same & (row >= col)
    strict = same & (row > col)
    first = lax.broadcasted_iota(jnp.int32, (pair, LANES), 0) < CHUNK

    l_mats = []
    for j in range(GDN_HG):
        h = hg * GDN_HG + j
        hl = slice(j * LANES, (j + 1) * LANES)

        def conv(x_ref, w_ref_, idx):
            x = x_ref[:, hl]
            y = _causal_conv_silu(x, tail_ref[idx, :, hl], w_ref_[j])
            tail_ref[idx, :, hl] = x[n - SUBLANES:n, :]
            return y

        q = conv(q_ref, wq_ref, 0)
        k = conv(k_ref, wk_ref, 1)
        v = conv(v_ref, wv_ref, 2)
        q = q * lax.rsqrt(jnp.sum(q * q, axis=-1, keepdims=True) + EPS) * (GDN_DK ** -0.5)
        k = k * lax.rsqrt(jnp.sum(k * k, axis=-1, keepdims=True) + EPS)
        a_col = jnp.sum(jnp.where(lane == SM_A + h, sm, 0.0), axis=-1, keepdims=True)
        b_col = jnp.sum(jnp.where(lane == SM_B + h, sm, 0.0), axis=-1, keepdims=True)
        log_decay = -jnp.exp(alog_ref[j]) * _softplus(a_col + dtb_ref[j])
        beta = _sigmoid(b_col)
        g = _seg_cumsum_rows(log_decay, CHUNK)
        eg = jnp.exp(g)
        kb = k * beta
        vb = v * beta

        rhs_ref[j, :, 0:GDN_DV] = vb.astype(BF16)
        rhs_ref[j, :, GDN_DV:] = (kb * eg).astype(BF16)
        qd_ref[j] = (q * eg).astype(BF16)
        for p in range(n // pair):
            sl = slice(p * pair, (p + 1) * pair)
            g_p = g[sl, :]
            g_rowmat = g_p.T
            decay = jnp.exp(jnp.where(causal, g_p - g_rowmat, -jnp.inf))
            k_p = k[sl, :].astype(BF16)
            kk = _dot_nt(kb[sl, :].astype(BF16), k_p)
            l_mats.append(jnp.where(strict, kk * decay, 0.0))
            qk = _dot_nt(q[sl, :].astype(BF16), k_p) * decay
            qk_ref[j, sl, :] = jnp.where(causal, qk, 0.0).astype(BF16)
            g_last = jnp.where(first, g_p[CHUNK - 1:CHUNK, :], g_p[pair - 1:pair, :])
            kd_ref[j, sl, :] = (k[sl, :] * jnp.exp(g_last - g_p)).astype(BF16)
            gl_ref[j, sl, :] = jnp.exp(g_last)

    x_invs = _unit_lower_inverses(l_mats, row, col)
    for idx, x_inv in enumerate(x_invs):
        j, p = divmod(idx, n // pair)
        sl = slice(p * pair, (p + 1) * pair)
        sol = _dot(x_inv.astype(BF16), rhs_ref[j, sl, :])
        u_ref[j, sl, :] = sol[:, 0:GDN_DV]
        w_ref[j, sl, :] = sol[:, GDN_DV:].astype(BF16)

    heads = range(GDN_HG)
    states = [state_ref[j] for j in heads]
    for c in range(n // CHUNK):
        sl = slice(c * CHUNK, (c + 1) * CHUNK)
        half = c % 2
        if half == 0:
            vn_ref[...] = jnp.zeros_like(vn_ref)
        sbs = [s.astype(BF16) for s in states]
        v_news = [u_ref[j, sl, :] - _dot(w_ref[j, sl, :], sbs[j]) for j in heads]
        vnbs = [v.astype(BF16) for v in v_news]
        for j in heads:
            vn_ref[j, half * CHUNK:(half + 1) * CHUNK, :] = vnbs[j]
        outs = [_dot(qd_ref[j, sl, :], sbs[j]) + _dot(qk_ref[j, sl, :], vn_ref[j]) for j in heads]
        states = [states[j] * gl_ref[j, c * CHUNK:c * CHUNK + 1, :] + _dot_tn(kd_ref[j, sl, :], vnbs[j])
                  for j in heads]
        for j in heads:
            hl = slice(j * LANES, (j + 1) * LANES)
            o = outs[j]
            ms = jnp.mean(o * o, axis=-1, keepdims=True)
            o = o * lax.rsqrt(ms + EPS) * gn_ref[...]
            o_ref[sl, hl] = (o * _silu(z_ref[sl, hl])).astype(o_ref.dtype)
    for j in heads:
        state_ref[j] = states[j]


def _gdn(proj, small, conv_w, a_log, dt_bias, gnorm, batch, seq):
    t = proj.shape[0]
    n = GDN_SBLK
    hgn = GDN_HG
    ns = seq // n
    cw = conv_w.reshape(CONV_K, 3 * GDN_HEADS, LANES).transpose(1, 0, 2)
    alog = jnp.broadcast_to(a_log.astype(F32)[:, None, None], (GDN_HEADS, 1, LANES))
    dtb = jnp.broadcast_to(dt_bias.astype(F32)[:, None, None], (GDN_HEADS, 1, LANES))

    def col(off):
        return pl.BlockSpec((n, hgn * LANES), lambda b, h, s: (b * ns + s, off // hgn + h))

    def cws(off):
        return pl.BlockSpec((hgn, CONV_K, LANES), lambda b, h, s: (off // hgn + h, 0, 0))

    par = pl.BlockSpec((hgn, 1, LANES), lambda b, h, s: (h, 0, 0))
    return pl.pallas_call(
        _gdn_kernel,
        grid=(batch, GDN_HEADS // hgn, ns),
        in_specs=[col(COL_GDN_Q), col(COL_GDN_K), col(COL_GDN_V), col(COL_GDN_Z),
                  pl.BlockSpec((n, LANES), lambda b, h, s: (b * ns + s, 0)),
                  cws(0), cws(GDN_HEADS), cws(2 * GDN_HEADS), par, par,
                  pl.BlockSpec((1, LANES), lambda b, h, s: (0, 0))],
        out_specs=pl.BlockSpec((n, hgn * LANES), lambda b, h, s: (b * ns + s, h)),
        out_shape=jax.ShapeDtypeStruct((t, BRANCH_WIDTH), BF16),
        scratch_shapes=[
            pltpu.VMEM((hgn, GDN_DK, GDN_DV), F32),
            pltpu.VMEM((3, SUBLANES, hgn * LANES), F32),
            pltpu.VMEM((hgn, n, GDN_DV + GDN_DK), BF16),
            pltpu.VMEM((hgn, n, LANES), F32),
            pltpu.VMEM((hgn, n, LANES), BF16),
            pltpu.VMEM((hgn, n, LANES), BF16),
            pltpu.VMEM((hgn, n, LANES), BF16),
            pltpu.VMEM((hgn, n, LANES), BF16),
            pltpu.VMEM((hgn, n, LANES), F32),
            pltpu.VMEM((hgn, 2 * CHUNK, LANES), BF16),
        ],
        compiler_params=_cparams(("parallel", "parallel", "arbitrary")),
        name="gdn",
    )(proj, proj, proj, proj, small, cw, cw, cw, alog, dtb, gnorm.reshape(1, LANES))


GLA_SBLK = 256


def _gla_kernel(q_ref, k_ref, v_ref, r_ref, sm_ref, wg_ref, bg_ref, gn_ref, o_ref, state_ref):
    n = GLA_SBLK

    @pl.when(pl.program_id(2) == 0)
    def _():
        state_ref[...] = jnp.zeros_like(state_ref)

    pre = _dot(sm_ref[...].astype(BF16), wg_ref[...]) + bg_ref[...]
    log_g = _log_sigmoid(pre) / GLA_NORMALIZER
    gcum = _seg_cumsum_rows(log_g, CHUNK)
    gcum_t = gcum.T
    q = q_ref[...] * (GLA_DK ** -0.5)
    k = k_ref[...]
    row = lax.broadcasted_iota(jnp.int32, (CHUNK, CHUNK), 0)
    col = lax.broadcasted_iota(jnp.int32, (CHUNK, CHUNK), 1)
    causal = row >= col

    for c in range(n // CHUNK):
        sl = slice(c * CHUNK, (c + 1) * CHUNK)
        g_c = gcum[sl, :]
        ref = g_c[CHUNK // 2 - 1:CHUNK // 2, :]
        g_end = g_c[CHUNK - 1:CHUNK, :]
        q_c, k_c = q[sl, :], k[sl, :]
        v_c = v_ref[sl, :].astype(BF16)
        a = _dot_nt((q_c * jnp.exp(g_c - ref)).astype(BF16), (k_c * jnp.exp(ref - g_c)).astype(BF16))
        a = jnp.where(causal, a, 0.0).astype(BF16)
        state = state_ref[...]
        o = _dot((q_c * jnp.exp(g_c)).astype(BF16), state.astype(BF16)) + _dot(a, v_c)
        k_dec = (k_c * jnp.exp(g_end - g_c)).astype(BF16)
        gl_col = jnp.exp(gcum_t[:, (c + 1) * CHUNK - 1:(c + 1) * CHUNK])
        state_ref[...] = state * gl_col + _dot_tn(k_dec, v_c)
        ms = jnp.mean(o * o, axis=-1, keepdims=True)
        o = o * lax.rsqrt(ms + EPS) * gn_ref[...]
        o_ref[sl, :] = (o * _silu(r_ref[sl, :])).astype(o_ref.dtype)


def _gla(proj, small, w_gate, b_gate, gnorm, batch, seq):
    t = proj.shape[0]
    n = GLA_SBLK
    ns = seq // n
    wg = jnp.zeros((LANES, GLA_HEADS * GLA_DK), BF16).at[SM_G:SM_G + GLA_RANK, :].set(w_gate.astype(BF16))
    rows = lambda b, h, s: b * ns + s
    return pl.pallas_call(
        _gla_kernel,
        grid=(batch, GLA_HEADS, ns),
        in_specs=[
            pl.BlockSpec((n, GLA_DK), lambda b, h, s: (rows(b, h, s), COL_GLA_Q + h)),
            pl.BlockSpec((n, GLA_DK), lambda b, h, s: (rows(b, h, s), COL_GLA_K + h)),
            pl.BlockSpec((n, GLA_DV), lambda b, h, s: (rows(b, h, s), COL_GLA_V // 2 + h)),
            pl.BlockSpec((n, GLA_DV), lambda b, h, s: (rows(b, h, s), COL_GLA_R // 2 + h)),
            pl.BlockSpec((n, LANES), lambda b, h, s: (rows(b, h, s), 0)),
            pl.BlockSpec((LANES, GLA_DK), lambda b, h, s: (0, h)),
            pl.BlockSpec((1, GLA_DK), lambda b, h, s: (0, h)),
            pl.BlockSpec((1, GLA_DV), lambda b, h, s: (0, 0)),
        ],
        out_specs=pl.BlockSpec((n, GLA_DV), lambda b, h, s: (rows(b, h, s), h)),
        out_shape=jax.ShapeDtypeStruct((t, BRANCH_WIDTH), BF16),
        scratch_shapes=[pltpu.VMEM((GLA_DK, GLA_DV), F32)],
        compiler_params=_cparams(("parallel", "parallel", "arbitrary")),
        name="gla",
    )(proj, proj, proj, proj, small, wg, b_gate.reshape(1, -1).astype(F32), gnorm.reshape(1, GLA_DV))


def _split_hi_lo(x):
    hi = x.astype(BF16)
    lo = (x - hi.astype(F32)).astype(BF16)
    return hi, lo


SB_TILE = 256
SB_EXIT = -104.0
SB_HG = 2


def _sb_kernel(q_ref, k_ref, v_ref, qg_ref, kg_ref, o_ref, kn_ref, vb_ref):
    qi = pl.program_id(2)
    tl = SB_TILE

    heads = range(SB_HG)
    hl = [slice(h * SB_DH, (h + 1) * SB_DH) for h in heads]

    @pl.when(qi == 0)
    def _():
        for h in heads:
            kf = k_ref[:, hl[h]]
            ms = jnp.mean(kf * kf, axis=-1, keepdims=True)
            kn_ref[:, hl[h]] = (kf * lax.rsqrt(ms + EPS) * kg_ref[...]).astype(BF16)
        vb_ref[...] = v_ref[...].astype(BF16)

    qns = []
    for h in heads:
        q = q_ref[:, hl[h]]
        ms = jnp.mean(q * q, axis=-1, keepdims=True)
        qns.append((q * lax.rsqrt(ms + EPS) * qg_ref[...]).astype(BF16))
    row = lax.broadcasted_iota(jnp.int32, (tl, tl), 0)
    col = lax.broadcasted_iota(jnp.int32, (tl, tl), 1)
    strict = col < row
    after = (row > col).astype(BF16)
    scale = SB_DH ** -0.5

    def tile(j, carries, accs, diag):
        off = pl.multiple_of(j * tl, tl)
        zs = [_dot_nt(qns[h], kn_ref[pl.ds(off, tl), hl[h]]) * scale for h in heads]
        log_betas = [jnp.minimum(z, 0.0) - jnp.log(1.0 + jnp.exp(-jnp.abs(z))) for z in zs]
        log_1ms = [lb - z for lb, z in zip(log_betas, zs)]
        if diag:
            log_1ms = [jnp.where(strict, l, 0.0) for l in log_1ms]
        parts = [_split_hi_lo(l) for l in log_1ms]
        tails = [_dot(hi, after) + _dot(lo, after) + c for (hi, lo), c in zip(parts, carries)]
        attns = [jnp.exp(lb + t) for lb, t in zip(log_betas, tails)]
        if diag:
            attns = [jnp.where(strict, a, 0.0) for a in attns]
        accs = [accs[h] + _dot(attns[h].astype(BF16), vb_ref[pl.ds(off, tl), hl[h]]) for h in heads]
        carries = [c + jnp.sum(l, axis=-1, keepdims=True) for c, l in zip(carries, log_1ms)]
        return carries, accs

    def unfinished(carries):
        worst = carries[0]
        for c in carries[1:]:
            worst = jnp.maximum(worst, c)
        return jnp.max(worst) > SB_EXIT

    carries, accs = tile(qi, [jnp.zeros((tl, 1), F32)] * SB_HG, [jnp.zeros((tl, SB_DH), F32)] * SB_HG, True)

    def more(state):
        return jnp.logical_and(state[0] >= 0, state[1])

    def body(state):
        j, _, carries, accs = state
        carries, accs = tile(j, list(carries), list(accs), False)
        return j - 1, unfinished(carries), tuple(carries), tuple(accs)

    state = lax.while_loop(more, body, (qi - 1, unfinished(carries), tuple(carries), tuple(accs)))
    for h in heads:
        o_ref[:, hl[h]] = state[3][h].astype(o_ref.dtype)


def _sb(proj, q_gain, k_gain, batch, seq):
    t = proj.shape[0]
    nq = seq // SB_TILE
    hw = SB_HG * SB_DH
    return pl.pallas_call(
        _sb_kernel,
        grid=(batch, SB_HEADS // SB_HG, nq),
        in_specs=[
            pl.BlockSpec((SB_TILE, hw), lambda b, h, i: (b * nq + i, COL_SB_Q // SB_HG + h)),
            pl.BlockSpec((seq, hw), lambda b, h, i: (b, COL_SB_K // SB_HG + h)),
            pl.BlockSpec((seq, hw), lambda b, h, i: (b, COL_SB_V // SB_HG + h)),
            pl.BlockSpec((1, SB_DH), lambda b, h, i: (0, 0)),
            pl.BlockSpec((1, SB_DH), lambda b, h, i: (0, 0)),
        ],
        out_specs=pl.BlockSpec((SB_TILE, hw), lambda b, h, i: (b * nq + i, h)),
        out_shape=jax.ShapeDtypeStruct((t, BRANCH_WIDTH), BF16),
        scratch_shapes=[pltpu.VMEM((seq, hw), BF16), pltpu.VMEM((seq, hw), BF16)],
        compiler_params=_cparams(("parallel", "parallel", "arbitrary")),
        name="stickbreak",
    )(proj, proj, proj, q_gain.reshape(1, SB_DH), k_gain.reshape(1, SB_DH))


def _merge_kernel(ya_ref, yb_ref, yc_ref, wb_ref, ga_ref, gb_ref, gc_ref, o_ref):
    acc = _sigmoid(ga_ref[...]) * _dot(ya_ref[...], wb_ref[0])
    acc = acc + _sigmoid(gb_ref[...]) * _dot(yb_ref[...], wb_ref[1])
    acc = acc + _sigmoid(gc_ref[...]) * _dot(yc_ref[...], wb_ref[2])
    o_ref[...] = acc.astype(o_ref.dtype)


def _merge(ya, yb, yc, proj, w_branch, *, tm=512, tn=512):
    t = ya.shape[0]
    tm = min(tm, t)
    nj = D_MODEL // tn
    g0 = COL_GATES * LANES // tn
    ysp = pl.BlockSpec((tm, BRANCH_WIDTH), lambda i, j: (i, 0))

    def gate(nb):
        return pl.BlockSpec((tm, tn), lambda i, j: (i, g0 + nb * nj + j))

    return pl.pallas_call(
        _merge_kernel,
        grid=(t // tm, nj),
        in_specs=[ysp, ysp, ysp,
                  pl.BlockSpec((N_BRANCH, BRANCH_WIDTH, tn), lambda i, j: (0, 0, j)),
                  gate(0), gate(1), gate(2)],
        out_specs=pl.BlockSpec((tm, tn), lambda i, j: (i, j)),
        out_shape=jax.ShapeDtypeStruct((t, D_MODEL), BF16),
        compiler_params=_cparams(("parallel", "arbitrary")),
        name="merge",
    )(ya, yb, yc, w_branch, proj, proj, proj)


def _out_proj_kernel(m_ref, w_ref, x_ref, g_ref, x1_ref, xn_ref):
    x1 = x_ref[...] + _dot(m_ref[...], w_ref[...])
    x1_ref[...] = x1
    ms = jnp.mean(x1 * x1, axis=-1, keepdims=True)
    xn_ref[...] = (x1 * lax.rsqrt(ms + EPS) * g_ref[...]).astype(BF16)


def _out_proj(merged, w_out, x2d, gain, *, tm=256):
    t, d = x2d.shape
    tm = min(tm, t)
    rows = pl.BlockSpec((tm, d), lambda i: (i, 0))
    return pl.pallas_call(
        _out_proj_kernel,
        grid=(t // tm,),
        in_specs=[rows, pl.BlockSpec((d, d), lambda i: (0, 0)), rows,
                  pl.BlockSpec((1, d), lambda i: (0, 0))],
        out_specs=[rows, rows],
        out_shape=[jax.ShapeDtypeStruct((t, d), F32), jax.ShapeDtypeStruct((t, d), BF16)],
        compiler_params=_cparams(("parallel",)),
        name="out_proj",
    )(merged, w_out, x2d, gain.reshape(1, d))


FFN_RB = 512
FFN_FC = 512


def _ffn_kernel(be_ref, valid_ref, x_ref, w1_ref, w3_ref, w2_ref, *rest, residual):
    if residual:
        res_ref, o_ref, acc_ref = rest
    else:
        o_ref, acc_ref = rest
    i, c = pl.program_id(0), pl.program_id(1)

    @pl.when(c == 0)
    def _():
        acc_ref[...] = jnp.zeros_like(acc_ref)

    @pl.when(valid_ref[i] == 1)
    def _():
        x = x_ref[...]
        h1 = _dot(x, w1_ref[...])
        h3 = _dot(x, w3_ref[...])
        acc_ref[...] += _dot((_silu(h1) * h3).astype(BF16), w2_ref[...])

    @pl.when(c == pl.num_programs(1) - 1)
    def _():
        y = acc_ref[...]
        if residual:
            y = y + res_ref[...]
        o_ref[...] = y.astype(o_ref.dtype)


def _ffn(x_rows, w1, w3, w2, block_expert, block_valid, residual=None, out_dtype=F32):
    r, d = x_rows.shape
    f = w1.shape[-1]
    rb = min(FFN_RB, r)
    nc = f // FFN_FC
    last = nc - 1

    def ccol(c, valid):
        return c * valid + last * (1 - valid)

    rows = pl.BlockSpec((rb, d), lambda i, c, be, va: (i, 0))
    in_specs = [
        rows,
        pl.BlockSpec((None, d, FFN_FC), lambda i, c, be, va: (be[i], 0, ccol(c, va[i]))),
        pl.BlockSpec((None, d, FFN_FC), lambda i, c, be, va: (be[i], 0, ccol(c, va[i]))),
        pl.BlockSpec((None, FFN_FC, d), lambda i, c, be, va: (be[i], ccol(c, va[i]), 0)),
    ]
    args = [x_rows, w1, w3, w2]
    if residual is not None:
        in_specs.append(rows)
        args.append(residual)
    return pl.pallas_call(
        functools.partial(_ffn_kernel, residual=residual is not None),
        grid_spec=pltpu.PrefetchScalarGridSpec(
            num_scalar_prefetch=2,
            grid=(r // rb, nc),
            in_specs=in_specs,
            out_specs=rows,
            scratch_shapes=[pltpu.VMEM((rb, d), F32)],
        ),
        out_shape=jax.ShapeDtypeStruct((r, d), out_dtype),
        compiler_params=_cparams(("parallel", "arbitrary")),
        name="swiglu",
    )(block_expert, block_valid, *args)


ROUTE_TT = 512
RT_W, RT_RANK, RT_MEMBER = 0, 8, 16


def _router_kernel(x_ref, g_ref, wr_ref, rowtab_ref, coltab_ref, cum_ref, count_ref):
    i = pl.program_id(0)
    tt = x_ref.shape[0]

    @pl.when(i == 0)
    def _():
        count_ref[...] = jnp.zeros_like(count_ref)

    x = x_ref[...]
    ms = jnp.mean(x * x, axis=-1, keepdims=True)
    xn = x * lax.rsqrt(ms + EPS) * g_ref[...]
    logits = jnp.dot(xn, wr_ref[...], preferred_element_type=F32, precision=lax.Precision.HIGHEST)
    lg = logits.T[0:N_EXPERTS, :]
    eidx = lax.broadcasted_iota(jnp.int32, lg.shape, 0).astype(F32)
    m1 = jnp.max(lg, axis=0, keepdims=True)
    i1 = jnp.min(jnp.where(lg == m1, eidx, float(N_EXPERTS)), axis=0, keepdims=True)
    sel1 = eidx == i1
    lg2 = jnp.where(sel1, -jnp.inf, lg)
    m2 = jnp.max(lg2, axis=0, keepdims=True)
    i2 = jnp.min(jnp.where(lg2 == m2, eidx, float(N_EXPERTS)), axis=0, keepdims=True)
    sel2 = eidx == i2
    e2 = jnp.exp(m2 - m1)
    den = 1.0 + e2
    wts = jnp.where(sel1, 1.0 / den, 0.0) + jnp.where(sel2, e2 / den, 0.0)
    member = (sel1 | sel2).astype(F32)

    srow = lax.broadcasted_iota(jnp.int32, (tt, tt), 0)
    scol = lax.broadcasted_iota(jnp.int32, (tt, tt), 1)
    before = (srow < scol).astype(BF16)
    base = count_ref[...]
    rank = _dot(member.astype(BF16), before) + base
    cum_ref[...] = jnp.broadcast_to(base, (N_EXPERTS, LANES))[None]
    count_ref[...] = base + jnp.sum(member, axis=1, keepdims=True)

    rowtab = jnp.concatenate([wts, rank, member], axis=0)
    rowtab_ref[...] = rowtab
    pad = jnp.zeros((LANES - 3 * N_EXPERTS, tt), F32)
    coltab_ref[...] = jnp.concatenate([rowtab, pad], axis=0).T


def _router(x1, gain, w_router):
    t, d = x1.shape
    tt = min(ROUTE_TT, t)
    nt = t // tt
    wr = jnp.zeros((d, LANES), F32).at[:, :N_EXPERTS].set(w_router.astype(F32))
    return pl.pallas_call(
        _router_kernel,
        grid=(nt,),
        in_specs=[pl.BlockSpec((tt, d), lambda i: (i, 0)),
                  pl.BlockSpec((1, d), lambda i: (0, 0)),
                  pl.BlockSpec((d, LANES), lambda i: (0, 0))],
        out_specs=[pl.BlockSpec((3 * N_EXPERTS, tt), lambda i: (0, i)),
                   pl.BlockSpec((tt, LANES), lambda i: (i, 0)),
                   pl.BlockSpec((1, N_EXPERTS, LANES), lambda i: (i, 0, 0))],
        out_shape=[jax.ShapeDtypeStruct((3 * N_EXPERTS, t), F32),
                   jax.ShapeDtypeStruct((t, LANES), F32),
                   jax.ShapeDtypeStruct((nt, N_EXPERTS, LANES), F32)],
        scratch_shapes=[pltpu.VMEM((N_EXPERTS, 1), F32)],
        compiler_params=_cparams(("arbitrary",)),
        name="router",
    )(x1, gain.reshape(1, d), wr)


def _gather_kernel(pb_ref, pt_ref, pe_ref, plo_ref, pfirst_ref, pskip_ref, x_ref, rt_ref, o_ref):
    g = pl.program_id(0)
    rb, tt = o_ref.shape[0], x_ref.shape[0]

    @pl.when(pfirst_ref[g] == 1)
    def _():
        o_ref[...] = jnp.zeros_like(o_ref)

    @pl.when(pskip_ref[g] == 0)
    def _():
        e = pe_ref[g]
        rt = rt_ref[...]
        ridx = lax.broadcasted_iota(jnp.int32, rt.shape, 0)
        rank = jnp.sum(jnp.where(ridx == RT_RANK + e, rt, 0.0), axis=0, keepdims=True)
        member = jnp.sum(jnp.where(ridx == RT_MEMBER + e, rt, 0.0), axis=0, keepdims=True)
        target = jnp.where(member > 0.5, rank - plo_ref[g].astype(F32), -1.0)
        rows = lax.broadcasted_iota(jnp.int32, (rb, tt), 0).astype(F32)
        onehot = (rows == target).astype(BF16)
        o_ref[...] += _dot(onehot, x_ref[...]).astype(o_ref.dtype)


def _gather_rows(xn, rowtab, pairs, nb):
    t, d = xn.shape
    tt = min(ROUTE_TT, t)
    rb = FFN_RB
    npairs = pairs[0].shape[0]
    return pl.pallas_call(
        _gather_kernel,
        grid_spec=pltpu.PrefetchScalarGridSpec(
            num_scalar_prefetch=6,
            grid=(npairs,),
            in_specs=[pl.BlockSpec((tt, d), lambda g, pb, pt, *_: (pt[g], 0)),
                      pl.BlockSpec((3 * N_EXPERTS, tt), lambda g, pb, pt, *_: (0, pt[g]))],
            out_specs=pl.BlockSpec((rb, d), lambda g, pb, *_: (pb[g], 0)),
        ),
        out_shape=jax.ShapeDtypeStruct((nb * rb, d), BF16),
        compiler_params=_cparams(("arbitrary",)),
        name="moe_gather",
    )(*pairs, xn, rowtab)


def _combine_kernel(pb_ref, pt_ref, pe_ref, plo_ref, pfirst_ref, pskip_ref, y_ref, ct_ref, x_ref,
                    o_ref):
    g = pl.program_id(0)
    tt, rb = o_ref.shape[0], y_ref.shape[0]

    @pl.when(pfirst_ref[g] == 1)
    def _():
        o_ref[...] = x_ref[...]

    @pl.when(pskip_ref[g] == 0)
    def _():
        e = pe_ref[g]
        ct = ct_ref[...]
        lane = lax.broadcasted_iota(jnp.int32, ct.shape, 1)
        wcol = jnp.sum(jnp.where(lane == RT_W + e, ct, 0.0), axis=1, keepdims=True)
        rank = jnp.sum(jnp.where(lane == RT_RANK + e, ct, 0.0), axis=1, keepdims=True)
        member = jnp.sum(jnp.where(lane == RT_MEMBER + e, ct, 0.0), axis=1, keepdims=True)
        target = jnp.where(member > 0.5, rank - plo_ref[g].astype(F32), -1.0)
        cols = lax.broadcasted_iota(jnp.int32, (tt, rb), 1).astype(F32)
        onehot = (cols == target).astype(BF16)
        o_ref[...] += wcol * _dot(onehot, y_ref[...])


def _combine(y_rows, coltab, x1, pairs):
    t, d = x1.shape
    tt = min(ROUTE_TT, t)
    rb = FFN_RB
    npairs = pairs[0].shape[0]
    return pl.pallas_call(
        _combine_kernel,
        grid_spec=pltpu.PrefetchScalarGridSpec(
            num_scalar_prefetch=6,
            grid=(npairs,),
            in_specs=[pl.BlockSpec((rb, d), lambda g, pb, *_: (pb[g], 0)),
                      pl.BlockSpec((tt, LANES), lambda g, pb, pt, *_: (pt[g], 0)),
                      pl.BlockSpec((tt, d), lambda g, pb, pt, *_: (pt[g], 0))],
            out_specs=pl.BlockSpec((tt, d), lambda g, pb, pt, *_: (pt[g], 0)),
        ),
        out_shape=jax.ShapeDtypeStruct((t, d), F32),
        compiler_params=_cparams(("arbitrary",)),
        name="moe_combine",
    )(*pairs, y_rows, coltab, x1)


def _pair_tables(cum, counts, nb, order):
    nt = cum.shape[0]
    rb = FFN_RB
    npairs = nb + N_EXPERTS * nt
    padded = (counts + rb - 1) // rb * rb
    seg_end = jnp.cumsum(padded)
    seg_start = seg_end - padded
    blk = jnp.arange(nb, dtype=jnp.int32)
    blk_row = blk * rb
    blk_e = jnp.minimum(jnp.searchsorted(seg_end, blk_row, side='right'), N_EXPERTS - 1).astype(jnp.int32)
    blk_lo = blk_row - seg_start[blk_e]
    blk_valid = (blk_row < seg_end[-1]) & (blk_lo < counts[blk_e])
    blk_hi = jnp.minimum(blk_lo + rb, counts[blk_e])
    tile_lo = cum[:, blk_e]
    tile_hi = jnp.concatenate([cum[1:], counts[None, :]], axis=0)[:, blk_e]
    overlap = blk_valid[None, :] & (tile_lo < blk_hi[None, :]) & (tile_hi > blk_lo[None, :]) \
        & (tile_hi > tile_lo)
    if order == 'block':
        flat = overlap.T.reshape(-1)
        idx = jnp.nonzero(flat, size=npairs, fill_value=-1)[0].astype(jnp.int32)
        n_valid = jnp.sum(flat.astype(jnp.int32))
        last = idx[jnp.maximum(n_valid - 1, 0)]
        idx = jnp.where(idx < 0, last, idx)
        pb, pt = idx // nt, idx % nt
        major = pb
    else:
        flat = overlap.reshape(-1)
        idx = jnp.nonzero(flat, size=npairs, fill_value=-1)[0].astype(jnp.int32)
        n_valid = jnp.sum(flat.astype(jnp.int32))
        last = idx[jnp.maximum(n_valid - 1, 0)]
        idx = jnp.where(idx < 0, last, idx)
        pt, pb = idx // nb, idx % nb
        major = pt
    pos = jnp.arange(npairs, dtype=jnp.int32)
    skip = (pos >= n_valid).astype(jnp.int32)
    first = jnp.concatenate([jnp.ones((1,), jnp.int32),
                             (major[1:] != major[:-1]).astype(jnp.int32)]) * (1 - skip)
    pairs = (pb.astype(jnp.int32), pt.astype(jnp.int32), blk_e[pb], blk_lo[pb].astype(jnp.int32),
             first.astype(jnp.int32), skip)
    return pairs, blk_e, blk_valid.astype(jnp.int32)


def _moe(x1, xn, gain, w_router, w1, w3, w2):
    t, d = x1.shape
    rowtab, coltab, cum3 = _router(x1, gain, w_router)
    cum = cum3[:, :, 0].astype(jnp.int32)
    tt = min(ROUTE_TT, t)
    last_members = jnp.sum(rowtab[RT_MEMBER:RT_MEMBER + N_EXPERTS, t - tt:], axis=1).astype(jnp.int32)
    counts = cum[-1] + last_members
    nb = (t * TOP_K) // FFN_RB + N_EXPERTS
    pairs_g, blk_e, blk_valid = _pair_tables(cum, counts, nb, 'block')
    pairs_c, _, _ = _pair_tables(cum, counts, nb, 'tile')
    x_rows = _gather_rows(xn, rowtab, pairs_g, nb)
    y_rows = _ffn(x_rows, w1, w3, w2, blk_e, blk_valid, out_dtype=BF16)
    return _combine(y_rows, coltab, x1, pairs_c)


def _split_w_in(w):
    w = w.astype(BF16)
    main = jnp.concatenate([w[:, 0:4096], w[:, 4112:7184], w[:, 7200:16416]], axis=1)
    small = jnp.concatenate([w[:, 4096:4112], w[:, 7184:7200],
                             jnp.zeros((w.shape[0], LANES - 32), BF16)], axis=1)
    return main, small


def kernel(x, norm_mix, w_in, conv_w, gdn_a_log, gdn_dt_bias, gdn_norm, gla_w_gate, gla_b_gate,
           gla_norm, sb_q_norm, sb_k_norm, w_branch, w_out, norm_ffn, ffn_w1, ffn_w3, ffn_w2,
           moe_router, moe_w1, moe_w3, moe_w2):
    batch, seq, d = x.shape
    t = batch * seq
    depth = w_in.shape[0]
    x2 = x.reshape(t, d)
    for layer in range(depth):
        w_main, w_small = _split_w_in(w_in[layer])
        proj, small = _in_proj(x2, norm_mix[layer], w_main, w_small)
        ya = _gdn(proj, small, conv_w[layer], gdn_a_log[layer], gdn_dt_bias[layer], gdn_norm[layer],
                  batch, seq)
        yb = _gla(proj, small, gla_w_gate[layer], gla_b_gate[layer], gla_norm[layer], batch, seq)
        yc = _sb(proj, sb_q_norm[layer], sb_k_norm[layer], batch, seq)
        merged = _merge(ya, yb, yc, proj, w_branch[layer].astype(BF16))
        x1, xn = _out_proj(merged, w_out[layer].astype(BF16), x2, norm_ffn[layer])
        i = layer // 2
        if layer % 2 == 0:
            nblk = t // min(FFN_RB, t)
            x2 = _ffn(xn, ffn_w1[i][None].astype(BF16), ffn_w3[i][None].astype(BF16),
                      ffn_w2[i][None].astype(BF16), jnp.zeros((nblk,), jnp.int32),
                      jnp.ones((nblk,), jnp.int32), residual=x1)
        else:
            x2 = _moe(x1, xn, norm_ffn[layer], moe_router[i], moe_w1[i].astype(BF16),
                      moe_w3[i].astype(BF16), moe_w2[i].astype(BF16))
    return x2.reshape(batch, seq, d)
```

```python
import functools

import jax
import jax.numpy as jnp
from jax import lax
from jax.experimental import pallas as pl
from jax.experimental.pallas import tpu as pltpu

F32 = jnp.float32
BF16 = jnp.bfloat16

D_MODEL = 2048
GDN_HEADS, GDN_DK, GDN_DV, CONV_K, CHUNK = 8, 128, 128, 4, 64
GLA_HEADS, GLA_DK, GLA_DV, GLA_RANK, GLA_NORMALIZER = 4, 128, 256, 16, 16.0
SB_HEADS, SB_DH, SB_BLOCK = 8, 128, 128
N_BRANCH, BRANCH_WIDTH = 3, 1024
D_FF, N_EXPERTS, TOP_K = 7168, 8, 2
EPS = 1e-6

LANES = 128
SUBLANES = 8
VMEM_LIMIT = 48 * 1024 * 1024

COL_GDN_Q, COL_GDN_K, COL_GDN_V, COL_GDN_Z = 0, 8, 16, 24
COL_GLA_Q, COL_GLA_K, COL_GLA_V, COL_GLA_R = 32, 36, 40, 48
COL_SB_Q, COL_SB_K, COL_SB_V = 56, 64, 72
COL_GATES = 80
N_MAIN = 128 * LANES
SM_A, SM_B, SM_G = 0, 8, 16


def _cparams(sem):
    return pltpu.CompilerParams(dimension_semantics=sem, vmem_limit_bytes=VMEM_LIMIT)


def _dot(a, b):
    return jnp.dot(a, b, preferred_element_type=F32)


def _dot_nt(a, b):
    return lax.dot_general(a, b, (((1,), (1,)), ((), ())), preferred_element_type=F32)


def _dot_tn(a, b):
    return lax.dot_general(a, b, (((0,), (0,)), ((), ())), preferred_element_type=F32)


def _sigmoid(x):
    return 1.0 / (1.0 + jnp.exp(-x))


def _silu(x):
    return x * _sigmoid(x)


def _softplus(x):
    return jnp.maximum(x, 0.0) + jnp.log1p(jnp.exp(-jnp.abs(x)))


def _log_sigmoid(x):
    return -_softplus(-x)


def _seg_cumsum_rows(x, seg):
    pos = lax.broadcasted_iota(jnp.int32, x.shape, 0) & (seg - 1)
    s = 1
    while s < seg:
        x = x + jnp.where(pos >= s, pltpu.roll(x, s, axis=0), 0.0)
        s *= 2
    return x


def _in_proj_kernel(x_ref, g_ref, wm_ref, ws_ref, proj_ref, small_ref, xn_ref):
    @pl.when(pl.program_id(1) == 0)
    def _():
        x = x_ref[...]
        ms = jnp.mean(x * x, axis=-1, keepdims=True)
        xn = (x * lax.rsqrt(ms + EPS) * g_ref[...]).astype(BF16)
        xn_ref[...] = xn
        small_ref[...] = _dot(xn, ws_ref[...])

    proj_ref[...] = _dot(xn_ref[...], wm_ref[...])


def _in_proj(x2d, gain, w_main, w_small, *, tm=1024, tn=1024):
    t, d = x2d.shape
    tm = min(tm, t)
    n = w_main.shape[1]
    return pl.pallas_call(
        _in_proj_kernel,
        grid=(t // tm, n // tn),
        in_specs=[
            pl.BlockSpec((tm, d), lambda i, j: (i, 0)),
            pl.BlockSpec((1, d), lambda i, j: (0, 0)),
            pl.BlockSpec((d, tn), lambda i, j: (0, j)),
            pl.BlockSpec((d, LANES), lambda i, j: (0, 0)),
        ],
        out_specs=[
            pl.BlockSpec((tm, tn), lambda i, j: (i, j)),
            pl.BlockSpec((tm, LANES), lambda i, j: (i, 0)),
        ],
        out_shape=[jax.ShapeDtypeStruct((t, n), F32), jax.ShapeDtypeStruct((t, LANES), F32)],
        scratch_shapes=[pltpu.VMEM((tm, d), BF16)],
        compiler_params=_cparams(("parallel", "arbitrary")),
        name="in_proj",
    )(x2d, gain.reshape(1, d), w_main, w_small)


GDN_SBLK = 256


def _causal_conv_silu(x, prev, w):
    n = x.shape[0]
    row8 = lax.broadcasted_iota(jnp.int32, (SUBLANES, LANES), 0)
    acc = x * w[CONV_K - 1:CONV_K, :]
    for j in range(1, CONV_K):
        xs = pltpu.roll(x, j, axis=0)
        ps = pltpu.roll(prev, j, axis=0)
        top = jnp.where(row8 < j, ps, xs[0:SUBLANES, :])
        xs = jnp.concatenate([top, xs[SUBLANES:n, :]], axis=0)
        acc = acc + xs * w[CONV_K - 1 - j:CONV_K - j, :]
    return _silu(acc)


def _unit_lower_inverses(l_mats, row, col):
    eye = (row == col).astype(F32)
    blk16 = (row >> 4) == (col >> 4)
    blk32 = (row >> 5) == (col >> 5)
    bf = lambda ms: [m.astype(BF16) for m in ms]
    mm = lambda xs, ys: [_dot(a, b) for a, b in zip(xs, ys)]

    d32 = [jnp.where(blk16, l, 0.0) for l in l_mats]
    d = bf(d32)
    e1 = bf([jnp.where(blk32 & (~blk16), l, 0.0) for l in l_mats])
    e2 = bf([jnp.where(blk32, 0.0, l) for l in l_mats])
    d2 = bf(mm(d, d))
    d4 = bf(mm(d2, d2))
    d8 = bf(mm(d4, d4))
    x = [eye - m for m in d32]
    for power in (d2, d4, d8):
        x = [a + b for a, b in zip(x, mm(bf(x), power))]
    for e in (e1, e2):
        xb = bf(x)
        x = [a - b for a, b in zip(x, mm(bf(mm(xb, e)), xb))]
    return x


GDN_HG = GDN_HEADS


def _gdn_kernel(q_ref, k_ref, v_ref, z_ref, sm_ref, wq_ref, wk_ref, wv_ref, alog_ref, dtb_ref,
                gn_ref, o_ref, state_ref, tail_ref, rhs_ref, u_ref, w_ref, qk_ref, qd_ref, kd_ref,
                gl_ref, vn_ref):
    n = GDN_SBLK
    pair = 2 * CHUNK

    @pl.when(pl.program_id(1) == 0)
    def _():
        state_ref[...] = jnp.zeros_like(state_ref)
        tail_ref[...] = jnp.zeros_like(tail_ref)

    sm = sm_ref[...]
    g_all = _seg_cumsum_rows(-jnp.exp(alog_ref[...]) * _softplus(sm + dtb_ref[...]), CHUNK)
    eg_all = jnp.exp(g_all)
    beta_all = _sigmoid(sm)
    row = lax.broadcasted_iota(jnp.int32, (pair, pair), 0)
    col = lax.broadcasted_iota(jnp.int32, (pair, pair), 1)
    same = (row >> 6) == (col >> 6)
    causal = same & (row >= col)
    strict = same & (row > col)
    first = lax.broadcasted_iota(jnp.int32, (pair, LANES), 0) < CHUNK

    l_mats = []
    for j in range(GDN_HG):
        hl = slice(j * LANES, (j + 1) * LANES)

        def conv(x_ref, w_ref_, idx):
            x = x_ref[:, hl]
            y = _causal_conv_silu(x, tail_ref[idx, :, hl], w_ref_[j])
            tail_ref[idx, :, hl] = x[n - SUBLANES:n, :]
            return y

        q = conv(q_ref, wq_ref, 0)
        k = conv(k_ref, wk_ref, 1)
        v = conv(v_ref, wv_ref, 2)
        q = q * lax.rsqrt(jnp.sum(q * q, axis=-1, keepdims=True) + EPS) * (GDN_DK ** -0.5)
        k = k * lax.rsqrt(jnp.sum(k * k, axis=-1, keepdims=True) + EPS)
        beta = beta_all[:, SM_B + j:SM_B + j + 1]
        eg = eg_all[:, SM_A + j:SM_A + j + 1]
        g = jnp.broadcast_to(g_all[:, SM_A + j:SM_A + j + 1], (n, LANES))
        kb = k * beta
        vb = v * beta

        rhs_ref[j, :, 0:GDN_DV] = vb.astype(BF16)
        rhs_ref[j, :, GDN_DV:] = (kb * eg).astype(BF16)
        qd_ref[j] = (q * eg).astype(BF16)
        for p in range(n // pair):
            sl = slice(p * pair, (p + 1) * pair)
            g_p = g[sl, :]
            g_rowmat = g_p.T
            decay = jnp.exp(jnp.where(causal, g_p - g_rowmat, -jnp.inf))
            k_p = k[sl, :].astype(BF16)
            kk = _dot_nt(kb[sl, :].astype(BF16), k_p)
            l_mats.append(jnp.where(strict, kk * decay, 0.0))
            qk = _dot_nt(q[sl, :].astype(BF16), k_p) * decay
            qk_ref[j, sl, :] = jnp.where(causal, qk, 0.0).astype(BF16)
            g_last = jnp.where(first, g_p[CHUNK - 1:CHUNK, :], g_p[pair - 1:pair, :])
            kd_ref[j, sl, :] = (k[sl, :] * jnp.exp(g_last - g_p)).astype(BF16)
            gl_ref[j, sl, :] = jnp.exp(g_last)

    x_invs = _unit_lower_inverses(l_mats, row, col)
    for idx, x_inv in enumerate(x_invs):
        j, p = divmod(idx, n // pair)
        sl = slice(p * pair, (p + 1) * pair)
        sol = _dot(x_inv.astype(BF16), rhs_ref[j, sl, :])
        u_ref[j, sl, :] = sol[:, 0:GDN_DV]
        w_ref[j, sl, :] = sol[:, GDN_DV:].astype(BF16)

    heads = range(GDN_HG)
    states = [state_ref[j] for j in heads]
    for c in range(n // CHUNK):
        sl = slice(c * CHUNK, (c + 1) * CHUNK)
        half = c % 2
        if half == 0:
            vn_ref[...] = jnp.zeros_like(vn_ref)
        sbs = [s.astype(BF16) for s in states]
        v_news = [u_ref[j, sl, :] - _dot(w_ref[j, sl, :], sbs[j]) for j in heads]
        vnbs = [v.astype(BF16) for v in v_news]
        for j in heads:
            vn_ref[j, half * CHUNK:(half + 1) * CHUNK, :] = vnbs[j]
        outs = [_dot(qd_ref[j, sl, :], sbs[j]) + _dot(qk_ref[j, sl, :], vn_ref[j]) for j in heads]
        states = [states[j] * gl_ref[j, c * CHUNK:c * CHUNK + 1, :] + _dot_tn(kd_ref[j, sl, :], vnbs[j])
                  for j in heads]
        for j in heads:
            hl = slice(j * LANES, (j + 1) * LANES)
            o = outs[j]
            ms = jnp.mean(o * o, axis=-1, keepdims=True)
            o = o * lax.rsqrt(ms + EPS) * gn_ref[...]
            o_ref[sl, hl] = (o * _silu(z_ref[sl, hl])).astype(o_ref.dtype)
    for j in heads:
        state_ref[j] = states[j]


def _gdn(proj, small, conv_w, a_log, dt_bias, gnorm, batch, seq):
    t = proj.shape[0]
    n = GDN_SBLK
    hgn = GDN_HG
    ns = seq // n
    cw = conv_w.reshape(CONV_K, 3 * GDN_HEADS, LANES).transpose(1, 0, 2)
    lane_pad = (SM_A, LANES - SM_A - GDN_HEADS)
    alog = jnp.pad(a_log.astype(F32), lane_pad).reshape(1, LANES)
    dtb = jnp.pad(dt_bias.astype(F32), lane_pad).reshape(1, LANES)

    def col(off):
        return pl.BlockSpec((n, hgn * LANES), lambda b, s: (b * ns + s, off // hgn))

    def cws(off):
        return pl.BlockSpec((hgn, CONV_K, LANES), lambda b, s: (off // hgn, 0, 0))

    par = pl.BlockSpec((1, LANES), lambda b, s: (0, 0))
    return pl.pallas_call(
        _gdn_kernel,
        grid=(batch, ns),
        in_specs=[col(COL_GDN_Q), col(COL_GDN_K), col(COL_GDN_V), col(COL_GDN_Z),
                  pl.BlockSpec((n, LANES), lambda b, s: (b * ns + s, 0)),
                  cws(0), cws(GDN_HEADS), cws(2 * GDN_HEADS), par, par, par],
        out_specs=pl.BlockSpec((n, hgn * LANES), lambda b, s: (b * ns + s, 0)),
        out_shape=jax.ShapeDtypeStruct((t, BRANCH_WIDTH), BF16),
        scratch_shapes=[
            pltpu.VMEM((hgn, GDN_DK, GDN_DV), F32),
            pltpu.VMEM((3, SUBLANES, hgn * LANES), F32),
            pltpu.VMEM((hgn, n, GDN_DV + GDN_DK), BF16),
            pltpu.VMEM((hgn, n, LANES), F32),
            pltpu.VMEM((hgn, n, LANES), BF16),
            pltpu.VMEM((hgn, n, LANES), BF16),
            pltpu.VMEM((hgn, n, LANES), BF16),
            pltpu.VMEM((hgn, n, LANES), BF16),
            pltpu.VMEM((hgn, n, LANES), F32),
            pltpu.VMEM((hgn, 2 * CHUNK, LANES), BF16),
        ],
        compiler_params=_cparams(("parallel", "arbitrary")),
        name="gdn",
    )(proj, proj, proj, proj, small, cw, cw, cw, alog, dtb, gnorm.reshape(1, LANES))


GLA_SBLK = 256


def _gla_kernel(q_ref, k_ref, v_ref, r_ref, sm_ref, wg_ref, bg_ref, gn_ref, o_ref, state_ref):
    n = GLA_SBLK
    heads = range(GLA_HEADS)
    chunks = range(n // CHUNK)

    @pl.when(pl.program_id(1) == 0)
    def _():
        state_ref[...] = jnp.zeros_like(state_ref)

    pre = _dot(sm_ref[...].astype(BF16), wg_ref[...]) + bg_ref[...]
    log_g = _log_sigmoid(pre) / GLA_NORMALIZER
    gcum = _seg_cumsum_rows(log_g, CHUNK)
    row = lax.broadcasted_iota(jnp.int32, (CHUNK, CHUNK), 0)
    col = lax.broadcasted_iota(jnp.int32, (CHUNK, CHUNK), 1)
    causal = row >= col

    intra, update, q_dec, g_last = {}, {}, {}, {}
    for h in heads:
        kl = slice(h * GLA_DK, (h + 1) * GLA_DK)
        vl = slice(h * GLA_DV, (h + 1) * GLA_DV)
        g_h = gcum[:, kl]
        g_t = g_h.T
        q = q_ref[:, kl] * (GLA_DK ** -0.5)
        k = k_ref[:, kl]
        for c in chunks:
            sl = slice(c * CHUNK, (c + 1) * CHUNK)
            g_c = g_h[sl, :]
            ref = g_c[CHUNK // 2 - 1:CHUNK // 2, :]
            g_end = g_c[CHUNK - 1:CHUNK, :]
            q_c, k_c = q[sl, :], k[sl, :]
            v_c = v_ref[sl, vl].astype(BF16)
            a = _dot_nt((q_c * jnp.exp(g_c - ref)).astype(BF16), (k_c * jnp.exp(ref - g_c)).astype(BF16))
            intra[h, c] = _dot(jnp.where(causal, a, 0.0).astype(BF16), v_c)
            update[h, c] = _dot_tn((k_c * jnp.exp(g_end - g_c)).astype(BF16), v_c)
            q_dec[h, c] = (q_c * jnp.exp(g_c)).astype(BF16)
            g_last[h, c] = jnp.exp(g_t[:, (c + 1) * CHUNK - 1:(c + 1) * CHUNK])

    for h in heads:
        vl = slice(h * GLA_DV, (h + 1) * GLA_DV)
        state = state_ref[h]
        for c in chunks:
            sl = slice(c * CHUNK, (c + 1) * CHUNK)
            o = _dot(q_dec[h, c], state.astype(BF16)) + intra[h, c]
            state = state * g_last[h, c] + update[h, c]
            ms = jnp.mean(o * o, axis=-1, keepdims=True)
            o = o * lax.rsqrt(ms + EPS) * gn_ref[...]
            o_ref[sl, vl] = (o * _silu(r_ref[sl, vl])).astype(o_ref.dtype)
        state_ref[h] = state


def _gla(proj, small, w_gate, b_gate, gnorm, batch, seq):
    t = proj.shape[0]
    n = GLA_SBLK
    ns = seq // n
    kw, vw = GLA_HEADS * GLA_DK, GLA_HEADS * GLA_DV
    wg = jnp.zeros((LANES, kw), BF16).at[SM_G:SM_G + GLA_RANK, :].set(w_gate.astype(BF16))
    rows = lambda b, s: b * ns + s
    return pl.pallas_call(
        _gla_kernel,
        grid=(batch, ns),
        in_specs=[
            pl.BlockSpec((n, kw), lambda b, s: (rows(b, s), COL_GLA_Q * LANES // kw)),
            pl.BlockSpec((n, kw), lambda b, s: (rows(b, s), COL_GLA_K * LANES // kw)),
            pl.BlockSpec((n, vw), lambda b, s: (rows(b, s), COL_GLA_V * LANES // vw)),
            pl.BlockSpec((n, vw), lambda b, s: (rows(b, s), COL_GLA_R * LANES // vw)),
            pl.BlockSpec((n, LANES), lambda b, s: (rows(b, s), 0)),
            pl.BlockSpec((LANES, kw), lambda b, s: (0, 0)),
            pl.BlockSpec((1, kw), lambda b, s: (0, 0)),
            pl.BlockSpec((1, GLA_DV), lambda b, s: (0, 0)),
        ],
        out_specs=pl.BlockSpec((n, vw), lambda b, s: (rows(b, s), 0)),
        out_shape=jax.ShapeDtypeStruct((t, BRANCH_WIDTH), BF16),
        scratch_shapes=[pltpu.VMEM((GLA_HEADS, GLA_DK, GLA_DV), F32)],
        compiler_params=_cparams(("parallel", "arbitrary")),
        name="gla",
    )(proj, proj, proj, proj, small, wg, b_gate.reshape(1, -1).astype(F32), gnorm.reshape(1, GLA_DV))


def _split_hi_lo(x):
    hi = x.astype(BF16)
    lo = (x - hi.astype(F32)).astype(BF16)
    return hi, lo


SB_TILE = 256
SB_EXIT = -104.0
SB_HG = 4


def _sb_kernel(q_ref, k_ref, v_ref, qg_ref, kg_ref, o_ref, kn_ref, vb_ref):
    qi = pl.program_id(2)
    tl = SB_TILE

    heads = range(SB_HG)
    hl = [slice(h * SB_DH, (h + 1) * SB_DH) for h in heads]

    @pl.when(qi == 0)
    def _():
        for h in heads:
            kf = k_ref[:, hl[h]]
            ms = jnp.mean(kf * kf, axis=-1, keepdims=True)
            kn_ref[:, hl[h]] = (kf * lax.rsqrt(ms + EPS) * kg_ref[...]).astype(BF16)
        vb_ref[...] = v_ref[...].astype(BF16)

    qns = []
    for h in heads:
        q = q_ref[:, hl[h]]
        ms = jnp.mean(q * q, axis=-1, keepdims=True)
        qns.append((q * lax.rsqrt(ms + EPS) * qg_ref[...]).astype(BF16))
    row = lax.broadcasted_iota(jnp.int32, (tl, tl), 0)
    col = lax.broadcasted_iota(jnp.int32, (tl, tl), 1)
    strict = col < row
    after = (row > col).astype(BF16)
    scale = SB_DH ** -0.5

    def tile(j, carries, accs, diag):
        off = pl.multiple_of(j * tl, tl)
        zs = [_dot_nt(qns[h], kn_ref[pl.ds(off, tl), hl[h]]) * scale for h in heads]
        log_betas = [jnp.minimum(z, 0.0) - jnp.log(1.0 + jnp.exp(-jnp.abs(z))) for z in zs]
        log_1ms = [lb - z for lb, z in zip(log_betas, zs)]
        if diag:
            log_1ms = [jnp.where(strict, l, 0.0) for l in log_1ms]
        parts = [_split_hi_lo(l) for l in log_1ms]
        tails = [_dot(hi, after) + _dot(lo, after) + c for (hi, lo), c in zip(parts, carries)]
        attns = [jnp.exp(lb + t) for lb, t in zip(log_betas, tails)]
        if diag:
            attns = [jnp.where(strict, a, 0.0) for a in attns]
        accs = [accs[h] + _dot(attns[h].astype(BF16), vb_ref[pl.ds(off, tl), hl[h]]) for h in heads]
        carries = [c + jnp.sum(l, axis=-1, keepdims=True) for c, l in zip(carries, log_1ms)]
        return carries, accs

    def unfinished(carries):
        worst = carries[0]
        for c in carries[1:]:
            worst = jnp.maximum(worst, c)
        return jnp.max(worst) > SB_EXIT

    carries, accs = tile(qi, [jnp.zeros((tl, 1), F32)] * SB_HG, [jnp.zeros((tl, SB_DH), F32)] * SB_HG, True)

    def more(state):
        return jnp.logical_and(state[0] >= 0, state[1])

    def body(state):
        j, _, carries, accs = state
        carries, accs = tile(j, list(carries), list(accs), False)
        return j - 1, unfinished(carries), tuple(carries), tuple(accs)

    state = lax.while_loop(more, body, (qi - 1, unfinished(carries), tuple(carries), tuple(accs)))
    for h in heads:
        o_ref[:, hl[h]] = state[3][h].astype(o_ref.dtype)


def _sb(proj, q_gain, k_gain, batch, seq):
    t = proj.shape[0]
    nq = seq // SB_TILE
    hw = SB_HG * SB_DH
    return pl.pallas_call(
        _sb_kernel,
        grid=(batch, SB_HEADS // SB_HG, nq),
        in_specs=[
            pl.BlockSpec((SB_TILE, hw), lambda b, h, i: (b * nq + i, COL_SB_Q // SB_HG + h)),
            pl.BlockSpec((seq, hw), lambda b, h, i: (b, COL_SB_K // SB_HG + h)),
            pl.BlockSpec((seq, hw), lambda b, h, i: (b, COL_SB_V // SB_HG + h)),
            pl.BlockSpec((1, SB_DH), lambda b, h, i: (0, 0)),
            pl.BlockSpec((1, SB_DH), lambda b, h, i: (0, 0)),
        ],
        out_specs=pl.BlockSpec((SB_TILE, hw), lambda b, h, i: (b * nq + i, h)),
        out_shape=jax.ShapeDtypeStruct((t, BRANCH_WIDTH), BF16),
        scratch_shapes=[pltpu.VMEM((seq, hw), BF16), pltpu.VMEM((seq, hw), BF16)],
        compiler_params=_cparams(("parallel", "parallel", "arbitrary")),
        name="stickbreak",
    )(proj, proj, proj, q_gain.reshape(1, SB_DH), k_gain.reshape(1, SB_DH))


def _merge_kernel(ya_ref, yb_ref, yc_ref, wb_ref, ga_ref, gb_ref, gc_ref, o_ref):
    acc = _sigmoid(ga_ref[...]) * _dot(ya_ref[...], wb_ref[0])
    acc = acc + _sigmoid(gb_ref[...]) * _dot(yb_ref[...], wb_ref[1])
    acc = acc + _sigmoid(gc_ref[...]) * _dot(yc_ref[...], wb_ref[2])
    o_ref[...] = acc.astype(o_ref.dtype)


def _merge(ya, yb, yc, proj, w_branch, *, tm=1024, tn=512):
    t = ya.shape[0]
    tm = min(tm, t)
    nj = D_MODEL // tn
    g0 = COL_GATES * LANES // tn
    ysp = pl.BlockSpec((tm, BRANCH_WIDTH), lambda i, j: (i, 0))

    def gate(nb):
        return pl.BlockSpec((tm, tn), lambda i, j: (i, g0 + nb * nj + j))

    return pl.pallas_call(
        _merge_kernel,
        grid=(t // tm, nj),
        in_specs=[ysp, ysp, ysp,
                  pl.BlockSpec((N_BRANCH, BRANCH_WIDTH, tn), lambda i, j: (0, 0, j)),
                  gate(0), gate(1), gate(2)],
        out_specs=pl.BlockSpec((tm, tn), lambda i, j: (i, j)),
        out_shape=jax.ShapeDtypeStruct((t, D_MODEL), BF16),
        compiler_params=_cparams(("parallel", "arbitrary")),
        name="merge",
    )(ya, yb, yc, w_branch, proj, proj, proj)


def _out_proj_kernel(m_ref, w_ref, x_ref, g_ref, x1_ref, xn_ref):
    x1 = x_ref[...] + _dot(m_ref[...], w_ref[...])
    x1_ref[...] = x1
    ms = jnp.mean(x1 * x1, axis=-1, keepdims=True)
    xn_ref[...] = (x1 * lax.rsqrt(ms + EPS) * g_ref[...]).astype(BF16)


def _out_proj(merged, w_out, x2d, gain, *, tm=256):
    t, d = x2d.shape
    tm = min(tm, t)
    rows = pl.BlockSpec((tm, d), lambda i: (i, 0))
    return pl.pallas_call(
        _out_proj_kernel,
        grid=(t // tm,),
        in_specs=[rows, pl.BlockSpec((d, d), lambda i: (0, 0)), rows,
                  pl.BlockSpec((1, d), lambda i: (0, 0))],
        out_specs=[rows, rows],
        out_shape=[jax.ShapeDtypeStruct((t, d), F32), jax.ShapeDtypeStruct((t, d), BF16)],
        compiler_params=_cparams(("parallel",)),
        name="out_proj",
    )(merged, w_out, x2d, gain.reshape(1, d))


FFN_RB = 512
FFN_FC = 512


def _ffn_kernel(be_ref, valid_ref, x_ref, w1_ref, w3_ref, w2_ref, *rest, residual):
    if residual:
        res_ref, o_ref, acc_ref = rest
    else:
        o_ref, acc_ref = rest
    i, c = pl.program_id(0), pl.program_id(1)

    @pl.when(c == 0)
    def _():
        acc_ref[...] = jnp.zeros_like(acc_ref)

    @pl.when(valid_ref[i] == 1)
    def _():
        x = x_ref[...]
        h1 = _dot(x, w1_ref[...])
        h3 = _dot(x, w3_ref[...])
        acc_ref[...] += _dot((_silu(h1) * h3).astype(BF16), w2_ref[...])

    @pl.when(c == pl.num_programs(1) - 1)
    def _():
        y = acc_ref[...]
        if residual:
            y = y + res_ref[...]
        o_ref[...] = y.astype(o_ref.dtype)


def _ffn(x_rows, w1, w3, w2, block_expert, block_valid, residual=None, out_dtype=F32):
    r, d = x_rows.shape
    f = w1.shape[-1]
    rb = min(FFN_RB, r)
    nc = f // FFN_FC
    last = nc - 1

    def ccol(c, valid):
        return c * valid + last * (1 - valid)

    rows = pl.BlockSpec((rb, d), lambda i, c, be, va: (i, 0))
    in_specs = [
        rows,
        pl.BlockSpec((None, d, FFN_FC), lambda i, c, be, va: (be[i], 0, ccol(c, va[i]))),
        pl.BlockSpec((None, d, FFN_FC), lambda i, c, be, va: (be[i], 0, ccol(c, va[i]))),
        pl.BlockSpec((None, FFN_FC, d), lambda i, c, be, va: (be[i], ccol(c, va[i]), 0)),
    ]
    args = [x_rows, w1, w3, w2]
    if residual is not None:
        in_specs.append(rows)
        args.append(residual)
    return pl.pallas_call(
        functools.partial(_ffn_kernel, residual=residual is not None),
        grid_spec=pltpu.PrefetchScalarGridSpec(
            num_scalar_prefetch=2,
            grid=(r // rb, nc),
            in_specs=in_specs,
            out_specs=rows,
            scratch_shapes=[pltpu.VMEM((rb, d), F32)],
        ),
        out_shape=jax.ShapeDtypeStruct((r, d), out_dtype),
        compiler_params=_cparams(("parallel", "arbitrary")),
        name="swiglu",
    )(block_expert, block_valid, *args)


ROUTE_TT = 512
RT_W, RT_RANK, RT_MEMBER = 0, 8, 16


def _router_kernel(x_ref, g_ref, wr_ref, rowtab_ref, coltab_ref, cum_ref, count_ref):
    i = pl.program_id(0)
    tt = x_ref.shape[0]

    @pl.when(i == 0)
    def _():
        count_ref[...] = jnp.zeros_like(count_ref)

    x = x_ref[...]
    ms = jnp.mean(x * x, axis=-1, keepdims=True)
    xn = x * lax.rsqrt(ms + EPS) * g_ref[...]
    logits = jnp.dot(xn, wr_ref[...], preferred_element_type=F32, precision=lax.Precision.HIGHEST)
    lg = logits.T[0:N_EXPERTS, :]
    eidx = lax.broadcasted_iota(jnp.int32, lg.shape, 0).astype(F32)
    m1 = jnp.max(lg, axis=0, keepdims=True)
    i1 = jnp.min(jnp.where(lg == m1, eidx, float(N_EXPERTS)), axis=0, keepdims=True)
    sel1 = eidx == i1
    lg2 = jnp.where(sel1, -jnp.inf, lg)
    m2 = jnp.max(lg2, axis=0, keepdims=True)
    i2 = jnp.min(jnp.where(lg2 == m2, eidx, float(N_EXPERTS)), axis=0, keepdims=True)
    sel2 = eidx == i2
    e2 = jnp.exp(m2 - m1)
    den = 1.0 + e2
    wts = jnp.where(sel1, 1.0 / den, 0.0) + jnp.where(sel2, e2 / den, 0.0)
    member = (sel1 | sel2).astype(F32)

    srow = lax.broadcasted_iota(jnp.int32, (tt, tt), 0)
    scol = lax.broadcasted_iota(jnp.int32, (tt, tt), 1)
    before = (srow < scol).astype(BF16)
    base = count_ref[...]
    rank = _dot(member.astype(BF16), before) + base
    cum_ref[...] = jnp.broadcast_to(base, (N_EXPERTS, LANES))[None]
    count_ref[...] = base + jnp.sum(member, axis=1, keepdims=True)

    rowtab = jnp.concatenate([wts, rank, member], axis=0)
    rowtab_ref[...] = rowtab
    pad = jnp.zeros((LANES - 3 * N_EXPERTS, tt), F32)
    coltab_ref[...] = jnp.concatenate([rowtab, pad], axis=0).T


def _router(x1, gain, w_router):
    t, d = x1.shape
    tt = min(ROUTE_TT, t)
    nt = t // tt
    wr = jnp.zeros((d, LANES), F32).at[:, :N_EXPERTS].set(w_router.astype(F32))
    return pl.pallas_call(
        _router_kernel,
        grid=(nt,),
        in_specs=[pl.BlockSpec((tt, d), lambda i: (i, 0)),
                  pl.BlockSpec((1, d), lambda i: (0, 0)),
                  pl.BlockSpec((d, LANES), lambda i: (0, 0))],
        out_specs=[pl.BlockSpec((3 * N_EXPERTS, tt), lambda i: (0, i)),
                   pl.BlockSpec((tt, LANES), lambda i: (i, 0)),
                   pl.BlockSpec((1, N_EXPERTS, LANES), lambda i: (i, 0, 0))],
        out_shape=[jax.ShapeDtypeStruct((3 * N_EXPERTS, t), F32),
                   jax.ShapeDtypeStruct((t, LANES), F32),
                   jax.ShapeDtypeStruct((nt, N_EXPERTS, LANES), F32)],
        scratch_shapes=[pltpu.VMEM((N_EXPERTS, 1), F32)],
        compiler_params=_cparams(("arbitrary",)),
        name="router",
    )(x1, gain.reshape(1, d), wr)


PAIR_WIN = 256


def _gather_kernel(pb_ref, pt_ref, pe_ref, plo_ref, pfirst_ref, pskip_ref, pwin_ref, x_ref, rt_ref,
                   o_ref):
    g = pl.program_id(0)
    rb, tt = o_ref.shape[0], x_ref.shape[0]
    win = pwin_ref[g]

    @pl.when(pfirst_ref[g] == 1)
    def _():
        o_ref[...] = jnp.zeros_like(o_ref)

    def target_row():
        e = pe_ref[g]
        rt = rt_ref[...]
        ridx = lax.broadcasted_iota(jnp.int32, rt.shape, 0)
        rank = jnp.sum(jnp.where(ridx == RT_RANK + e, rt, 0.0), axis=0, keepdims=True)
        member = jnp.sum(jnp.where(ridx == RT_MEMBER + e, rt, 0.0), axis=0, keepdims=True)
        return jnp.where(member > 0.5, rank - plo_ref[g].astype(F32), -1.0)

    @pl.when((pskip_ref[g] == 0) & (win >= 0))
    def _():
        w0 = pl.multiple_of(win, LANES)
        rows = (lax.broadcasted_iota(jnp.int32, (PAIR_WIN, tt), 0) + w0).astype(F32)
        onehot = (rows == target_row()).astype(BF16)
        o_ref[pl.ds(w0, PAIR_WIN), :] += _dot(onehot, x_ref[...]).astype(o_ref.dtype)

    @pl.when((pskip_ref[g] == 0) & (win < 0))
    def _():
        rows = lax.broadcasted_iota(jnp.int32, (rb, tt), 0).astype(F32)
        onehot = (rows == target_row()).astype(BF16)
        o_ref[...] += _dot(onehot, x_ref[...]).astype(o_ref.dtype)


def _gather_rows(xn, rowtab, pairs, nb):
    t, d = xn.shape
    tt = min(ROUTE_TT, t)
    rb = FFN_RB
    npairs = pairs[0].shape[0]
    return pl.pallas_call(
        _gather_kernel,
        grid_spec=pltpu.PrefetchScalarGridSpec(
            num_scalar_prefetch=7,
            grid=(npairs,),
            in_specs=[pl.BlockSpec((tt, d), lambda g, pb, pt, *_: (pt[g], 0)),
                      pl.BlockSpec((3 * N_EXPERTS, tt), lambda g, pb, pt, *_: (0, pt[g]))],
            out_specs=pl.BlockSpec((rb, d), lambda g, pb, *_: (pb[g], 0)),
        ),
        out_shape=jax.ShapeDtypeStruct((nb * rb, d), BF16),
        compiler_params=_cparams(("arbitrary",)),
        name="moe_gather",
    )(*pairs, xn, rowtab)


def _combine_kernel(pb_ref, pt_ref, pe_ref, plo_ref, pfirst_ref, pskip_ref, pwin_ref, y_ref, ct_ref,
                    x_ref, o_ref):
    g = pl.program_id(0)
    tt, rb = o_ref.shape[0], y_ref.shape[0]
    win = pwin_ref[g]

    @pl.when(pfirst_ref[g] == 1)
    def _():
        o_ref[...] = x_ref[...]

    def weight_and_target():
        e = pe_ref[g]
        ct = ct_ref[...]
        lane = lax.broadcasted_iota(jnp.int32, ct.shape, 1)
        wcol = jnp.sum(jnp.where(lane == RT_W + e, ct, 0.0), axis=1, keepdims=True)
        rank = jnp.sum(jnp.where(lane == RT_RANK + e, ct, 0.0), axis=1, keepdims=True)
        member = jnp.sum(jnp.where(lane == RT_MEMBER + e, ct, 0.0), axis=1, keepdims=True)
        return wcol, jnp.where(member > 0.5, rank - plo_ref[g].astype(F32), -1.0)

    @pl.when((pskip_ref[g] == 0) & (win >= 0))
    def _():
        w0 = pl.multiple_of(win, LANES)
        wcol, target = weight_and_target()
        cols = (lax.broadcasted_iota(jnp.int32, (tt, PAIR_WIN), 1) + w0).astype(F32)
        onehot = (cols == target).astype(BF16)
        o_ref[...] += wcol * _dot(onehot, y_ref[pl.ds(w0, PAIR_WIN), :])

    @pl.when((pskip_ref[g] == 0) & (win < 0))
    def _():
        wcol, target = weight_and_target()
        cols = lax.broadcasted_iota(jnp.int32, (tt, rb), 1).astype(F32)
        onehot = (cols == target).astype(BF16)
        o_ref[...] += wcol * _dot(onehot, y_ref[...])


def _combine(y_rows, coltab, x1, pairs):
    t, d = x1.shape
    tt = min(ROUTE_TT, t)
    rb = FFN_RB
    npairs = pairs[0].shape[0]
    return pl.pallas_call(
        _combine_kernel,
        grid_spec=pltpu.PrefetchScalarGridSpec(
            num_scalar_prefetch=7,
            grid=(npairs,),
            in_specs=[pl.BlockSpec((rb, d), lambda g, pb, *_: (pb[g], 0)),
                      pl.BlockSpec((tt, LANES), lambda g, pb, pt, *_: (pt[g], 0)),
                      pl.BlockSpec((tt, d), lambda g, pb, pt, *_: (pt[g], 0))],
            out_specs=pl.BlockSpec((tt, d), lambda g, pb, pt, *_: (pt[g], 0)),
        ),
        out_shape=jax.ShapeDtypeStruct((t, d), F32),
        compiler_params=_cparams(("arbitrary",)),
        name="moe_combine",
    )(*pairs, y_rows, coltab, x1)


def _pair_tables(cum, counts, nb, order):
    nt = cum.shape[0]
    rb = FFN_RB
    npairs = nb + N_EXPERTS * nt
    padded = (counts + rb - 1) // rb * rb
    seg_end = jnp.cumsum(padded)
    seg_start = seg_end - padded
    blk = jnp.arange(nb, dtype=jnp.int32)
    blk_row = blk * rb
    blk_e = jnp.minimum(jnp.searchsorted(seg_end, blk_row, side='right'), N_EXPERTS - 1).astype(jnp.int32)
    blk_lo = blk_row - seg_start[blk_e]
    blk_valid = (blk_row < seg_end[-1]) & (blk_lo < counts[blk_e])
    blk_hi = jnp.minimum(blk_lo + rb, counts[blk_e])
    tile_lo = cum[:, blk_e]
    tile_hi = jnp.concatenate([cum[1:], counts[None, :]], axis=0)[:, blk_e]
    overlap = blk_valid[None, :] & (tile_lo < blk_hi[None, :]) & (tile_hi > blk_lo[None, :]) \
        & (tile_hi > tile_lo)
    if order == 'block':
        flat = overlap.T.reshape(-1)
        idx = jnp.nonzero(flat, size=npairs, fill_value=-1)[0].astype(jnp.int32)
        n_valid = jnp.sum(flat.astype(jnp.int32))
        last = idx[jnp.maximum(n_valid - 1, 0)]
        idx = jnp.where(idx < 0, last, idx)
        pb, pt = idx // nt, idx % nt
        major = pb
    else:
        flat = overlap.reshape(-1)
        idx = jnp.nonzero(flat, size=npairs, fill_value=-1)[0].astype(jnp.int32)
        n_valid = jnp.sum(flat.astype(jnp.int32))
        last = idx[jnp.maximum(n_valid - 1, 0)]
        idx = jnp.where(idx < 0, last, idx)
        pt, pb = idx // nb, idx % nb
        major = pt
    pos = jnp.arange(npairs, dtype=jnp.int32)
    skip = (pos >= n_valid).astype(jnp.int32)
    first = jnp.concatenate([jnp.ones((1,), jnp.int32),
                             (major[1:] != major[:-1]).astype(jnp.int32)]) * (1 - skip)
    r0 = jnp.maximum(tile_lo[pt, pb] - blk_lo[pb], 0)
    r1 = jnp.minimum(tile_hi[pt, pb], blk_hi[pb]) - blk_lo[pb]
    w0 = jnp.minimum(r0 // LANES * LANES, rb - PAIR_WIN)
    win = jnp.where(r1 <= w0 + PAIR_WIN, w0, -1)
    pairs = (pb.astype(jnp.int32), pt.astype(jnp.int32), blk_e[pb], blk_lo[pb].astype(jnp.int32),
             first.astype(jnp.int32), skip, win.astype(jnp.int32))
    return pairs, blk_e, blk_valid.astype(jnp.int32)


def _moe(x1, xn, gain, w_router, w1, w3, w2):
    t, d = x1.shape
    rowtab, coltab, cum3 = _router(x1, gain, w_router)
    cum = cum3[:, :, 0].astype(jnp.int32)
    tt = min(ROUTE_TT, t)
    last_members = jnp.sum(rowtab[RT_MEMBER:RT_MEMBER + N_EXPERTS, t - tt:], axis=1).astype(jnp.int32)
    counts = cum[-1] + last_members
    nb = (t * TOP_K) // FFN_RB + N_EXPERTS
    pairs_g, blk_e, blk_valid = _pair_tables(cum, counts, nb, 'block')
    pairs_c, _, _ = _pair_tables(cum, counts, nb, 'tile')
    x_rows = _gather_rows(xn, rowtab, pairs_g, nb)
    y_rows = _ffn(x_rows, w1, w3, w2, blk_e, blk_valid, out_dtype=BF16)
    return _combine(y_rows, coltab, x1, pairs_c)


def _split_w_in(w):
    main = jnp.concatenate([w[:, 0:4096], w[:, 4112:7184], w[:, 7200:16416]], axis=1).astype(BF16)
    small = jnp.concatenate([w[:, 4096:4112], w[:, 7184:7200],
                             jnp.zeros((w.shape[0], LANES - 32), w.dtype)], axis=1).astype(BF16)
    return main, small


def kernel(x, norm_mix, w_in, conv_w, gdn_a_log, gdn_dt_bias, gdn_norm, gla_w_gate, gla_b_gate,
           gla_norm, sb_q_norm, sb_k_norm, w_branch, w_out, norm_ffn, ffn_w1, ffn_w3, ffn_w2,
           moe_router, moe_w1, moe_w3, moe_w2):
    batch, seq, d = x.shape
    t = batch * seq
    depth = w_in.shape[0]
    x2 = x.reshape(t, d)
    for layer in range(depth):
        w_main, w_small = _split_w_in(w_in[layer])
        proj, small = _in_proj(x2, norm_mix[layer], w_main, w_small)
        ya = _gdn(proj, small, conv_w[layer], gdn_a_log[layer], gdn_dt_bias[layer], gdn_norm[layer],
                  batch, seq)
        yb = _gla(proj, small, gla_w_gate[layer], gla_b_gate[layer], gla_norm[layer], batch, seq)
        yc = _sb(proj, sb_q_norm[layer], sb_k_norm[layer], batch, seq)
        merged = _merge(ya, yb, yc, proj, w_branch[layer].astype(BF16))
        x1, xn = _out_proj(merged, w_out[layer].astype(BF16), x2, norm_ffn[layer])
        i = layer // 2
        if layer % 2 == 0:
            nblk = t // min(FFN_RB, t)
            x2 = _ffn(xn, ffn_w1[i][None].astype(BF16), ffn_w3[i][None].astype(BF16),
                      ffn_w2[i][None].astype(BF16), jnp.zeros((nblk,), jnp.int32),
                      jnp.ones((nblk,), jnp.int32), residual=x1)
        else:
            x2 = _moe(x1, xn, norm_ffn[layer], moe_router[i], moe_w1[i].astype(BF16),
                      moe_w3[i].astype(BF16), moe_w2[i].astype(BF16))
    return x2.reshape(batch, seq, d)
```

```python
import functools

import jax
import jax.numpy as jnp
from jax import lax
from jax.experimental import pallas as pl
from jax.experimental.pallas import tpu as pltpu

F32 = jnp.float32
BF16 = jnp.bfloat16

D_MODEL = 2048
GDN_HEADS, GDN_DK, GDN_DV, CONV_K, CHUNK = 8, 128, 128, 4, 64
GLA_HEADS, GLA_DK, GLA_DV, GLA_RANK, GLA_NORMALIZER = 4, 128, 256, 16, 16.0
SB_HEADS, SB_DH, SB_BLOCK = 8, 128, 128
N_BRANCH, BRANCH_WIDTH = 3, 1024
D_FF, N_EXPERTS, TOP_K = 7168, 8, 2
EPS = 1e-6

LANES = 128
SUBLANES = 8
VMEM_LIMIT = 48 * 1024 * 1024

COL_GDN_Q, COL_GDN_K, COL_GDN_V, COL_GDN_Z = 0, 8, 16, 24
COL_GLA_Q, COL_GLA_K, COL_GLA_V, COL_GLA_R = 32, 36, 40, 48
COL_SB_Q, COL_SB_K, COL_SB_V = 56, 64, 72
COL_GATES = 80
N_MAIN = 128 * LANES
SM_A, SM_B, SM_G = 0, 8, 16


def _cparams(sem):
    return pltpu.CompilerParams(dimension_semantics=sem, vmem_limit_bytes=VMEM_LIMIT)


def _dot(a, b):
    return jnp.dot(a, b, preferred_element_type=F32)


def _dot_nt(a, b):
    return lax.dot_general(a, b, (((1,), (1,)), ((), ())), preferred_element_type=F32)


def _dot_tn(a, b):
    return lax.dot_general(a, b, (((0,), (0,)), ((), ())), preferred_element_type=F32)


def _sigmoid(x):
    return 1.0 / (1.0 + jnp.exp(-x))


def _silu(x):
    return x * _sigmoid(x)


def _softplus(x):
    return jnp.maximum(x, 0.0) + jnp.log1p(jnp.exp(-jnp.abs(x)))


def _log_sigmoid(x):
    return -_softplus(-x)


def _seg_cumsum_rows(x, seg):
    pos = lax.broadcasted_iota(jnp.int32, x.shape, 0) & (seg - 1)
    s = 1
    while s < seg:
        x = x + jnp.where(pos >= s, pltpu.roll(x, s, axis=0), 0.0)
        s *= 2
    return x


def _in_proj_kernel(x_ref, g_ref, wm_ref, ws_ref, proj_ref, small_ref, xn_ref):
    @pl.when(pl.program_id(1) == 0)
    def _():
        x = x_ref[...]
        ms = jnp.mean(x * x, axis=-1, keepdims=True)
        xn = (x * lax.rsqrt(ms + EPS) * g_ref[...]).astype(BF16)
        xn_ref[...] = xn
        small_ref[...] = _dot(xn, ws_ref[...])

    proj_ref[...] = _dot(xn_ref[...], wm_ref[...])


def _in_proj(x2d, gain, w_main, w_small, *, tm=1024, tn=1024):
    t, d = x2d.shape
    tm = min(tm, t)
    n = w_main.shape[1]
    return pl.pallas_call(
        _in_proj_kernel,
        grid=(t // tm, n // tn),
        in_specs=[
            pl.BlockSpec((tm, d), lambda i, j: (i, 0)),
            pl.BlockSpec((1, d), lambda i, j: (0, 0)),
            pl.BlockSpec((d, tn), lambda i, j: (0, j)),
            pl.BlockSpec((d, LANES), lambda i, j: (0, 0)),
        ],
        out_specs=[
            pl.BlockSpec((tm, tn), lambda i, j: (i, j)),
            pl.BlockSpec((tm, LANES), lambda i, j: (i, 0)),
        ],
        out_shape=[jax.ShapeDtypeStruct((t, n), F32), jax.ShapeDtypeStruct((t, LANES), F32)],
        scratch_shapes=[pltpu.VMEM((tm, d), BF16)],
        compiler_params=_cparams(("parallel", "arbitrary")),
        name="in_proj",
    )(x2d, gain.reshape(1, d), w_main, w_small)


GDN_SBLK = 256


def _causal_conv_silu(x, prev, w):
    n = x.shape[0]
    row8 = lax.broadcasted_iota(jnp.int32, (SUBLANES, LANES), 0)
    acc = x * w[CONV_K - 1:CONV_K, :]
    for j in range(1, CONV_K):
        xs = pltpu.roll(x, j, axis=0)
        ps = pltpu.roll(prev, j, axis=0)
        top = jnp.where(row8 < j, ps, xs[0:SUBLANES, :])
        xs = jnp.concatenate([top, xs[SUBLANES:n, :]], axis=0)
        acc = acc + xs * w[CONV_K - 1 - j:CONV_K - j, :]
    return _silu(acc)


def _unit_lower_inverses(l_mats, row, col):
    eye = (row == col).astype(F32)
    blk16 = (row >> 4) == (col >> 4)
    blk32 = (row >> 5) == (col >> 5)
    bf = lambda ms: [m.astype(BF16) for m in ms]
    mm = lambda xs, ys: [_dot(a, b) for a, b in zip(xs, ys)]

    d32 = [jnp.where(blk16, l, 0.0) for l in l_mats]
    d = bf(d32)
    e1 = bf([jnp.where(blk32 & (~blk16), l, 0.0) for l in l_mats])
    e2 = bf([jnp.where(blk32, 0.0, l) for l in l_mats])
    d2 = bf(mm(d, d))
    d4 = bf(mm(d2, d2))
    d8 = bf(mm(d4, d4))
    x = [eye - m for m in d32]
    for power in (d2, d4, d8):
        x = [a + b for a, b in zip(x, mm(bf(x), power))]
    for e in (e1, e2):
        xb = bf(x)
        x = [a - b for a, b in zip(x, mm(bf(mm(xb, e)), xb))]
    return x


GDN_HG = GDN_HEADS


def _gdn_kernel(q_ref, k_ref, v_ref, z_ref, sm_ref, wq_ref, wk_ref, wv_ref, alog_ref, dtb_ref,
                gn_ref, o_ref, state_ref, tail_ref, rhs_ref, u_ref, w_ref, qk_ref, qd_ref, kd_ref,
                gl_ref, vn_ref):
    n = GDN_SBLK
    pair = 2 * CHUNK

    @pl.when(pl.program_id(1) == 0)
    def _():
        state_ref[...] = jnp.zeros_like(state_ref)
        tail_ref[...] = jnp.zeros_like(tail_ref)

    sm = sm_ref[...]
    g_all = _seg_cumsum_rows(-jnp.exp(alog_ref[...]) * _softplus(sm + dtb_ref[...]), CHUNK)
    eg_all = jnp.exp(g_all)
    beta_all = _sigmoid(sm)
    row = lax.broadcasted_iota(jnp.int32, (pair, pair), 0)
    col = lax.broadcasted_iota(jnp.int32, (pair, pair), 1)
    same = (row >> 6) == (col >> 6)
    causal = same & (row >= col)
    strict = same & (row > col)
    first = lax.broadcasted_iota(jnp.int32, (pair, LANES), 0) < CHUNK

    l_mats = []
    for j in range(GDN_HG):
        hl = slice(j * LANES, (j + 1) * LANES)

        def conv(x_ref, w_ref_, idx):
            x = x_ref[:, hl]
            y = _causal_conv_silu(x, tail_ref[idx, :, hl], w_ref_[j])
            tail_ref[idx, :, hl] = x[n - SUBLANES:n, :]
            return y

        q = conv(q_ref, wq_ref, 0)
        k = conv(k_ref, wk_ref, 1)
        v = conv(v_ref, wv_ref, 2)
        q = q * lax.rsqrt(jnp.sum(q * q, axis=-1, keepdims=True) + EPS) * (GDN_DK ** -0.5)
        k = k * lax.rsqrt(jnp.sum(k * k, axis=-1, keepdims=True) + EPS)
        beta = beta_all[:, SM_B + j:SM_B + j + 1]
        eg = eg_all[:, SM_A + j:SM_A + j + 1]
        g = jnp.broadcast_to(g_all[:, SM_A + j:SM_A + j + 1], (n, LANES))
        kb = k * beta
        vb = v * beta

        rhs_ref[j, :, 0:GDN_DV] = vb.astype(BF16)
        rhs_ref[j, :, GDN_DV:] = (kb * eg).astype(BF16)
        qd_ref[j] = (q * eg).astype(BF16)
        for p in range(n // pair):
            sl = slice(p * pair, (p + 1) * pair)
            g_p = g[sl, :]
            g_rowmat = g_p.T
            decay = jnp.exp(jnp.where(causal, g_p - g_rowmat, -jnp.inf))
            k_p = k[sl, :].astype(BF16)
            kk = _dot_nt(kb[sl, :].astype(BF16), k_p)
            l_mats.append(jnp.where(strict, kk * decay, 0.0))
            qk = _dot_nt(q[sl, :].astype(BF16), k_p) * decay
            qk_ref[j, sl, :] = jnp.where(causal, qk, 0.0).astype(BF16)
            g_last = jnp.where(first, g_p[CHUNK - 1:CHUNK, :], g_p[pair - 1:pair, :])
            kd_ref[j, sl, :] = (k[sl, :] * jnp.exp(g_last - g_p)).astype(BF16)
            gl_ref[j, sl, :] = jnp.exp(g_last)

    x_invs = _unit_lower_inverses(l_mats, row, col)
    for idx, x_inv in enumerate(x_invs):
        j, p = divmod(idx, n // pair)
        sl = slice(p * pair, (p + 1) * pair)
        sol = _dot(x_inv.astype(BF16), rhs_ref[j, sl, :])
        u_ref[j, sl, :] = sol[:, 0:GDN_DV]
        w_ref[j, sl, :] = sol[:, GDN_DV:].astype(BF16)

    heads = range(GDN_HG)
    states = [state_ref[j] for j in heads]
    for c in range(n // CHUNK):
        sl = slice(c * CHUNK, (c + 1) * CHUNK)
        half = c % 2
        if half == 0:
            vn_ref[...] = jnp.zeros_like(vn_ref)
        sbs = [s.astype(BF16) for s in states]
        v_news = [u_ref[j, sl, :] - _dot(w_ref[j, sl, :], sbs[j]) for j in heads]
        vnbs = [v.astype(BF16) for v in v_news]
        for j in heads:
            vn_ref[j, half * CHUNK:(half + 1) * CHUNK, :] = vnbs[j]
        outs = [_dot(qd_ref[j, sl, :], sbs[j]) + _dot(qk_ref[j, sl, :], vn_ref[j]) for j in heads]
        states = [states[j] * gl_ref[j, c * CHUNK:c * CHUNK + 1, :] + _dot_tn(kd_ref[j, sl, :], vnbs[j])
                  for j in heads]
        for j in heads:
            hl = slice(j * LANES, (j + 1) * LANES)
            o = outs[j]
            ms = jnp.mean(o * o, axis=-1, keepdims=True)
            o = o * lax.rsqrt(ms + EPS) * gn_ref[...]
            o_ref[sl, hl] = (o * _silu(z_ref[sl, hl])).astype(o_ref.dtype)
    for j in heads:
        state_ref[j] = states[j]


def _gdn(proj, small, conv_w, a_log, dt_bias, gnorm, batch, seq):
    t = proj.shape[0]
    n = GDN_SBLK
    hgn = GDN_HG
    ns = seq // n
    cw = conv_w.reshape(CONV_K, 3 * GDN_HEADS, LANES).transpose(1, 0, 2)
    lane_pad = (SM_A, LANES - SM_A - GDN_HEADS)
    alog = jnp.pad(a_log.astype(F32), lane_pad).reshape(1, LANES)
    dtb = jnp.pad(dt_bias.astype(F32), lane_pad).reshape(1, LANES)

    def col(off):
        return pl.BlockSpec((n, hgn * LANES), lambda b, s: (b * ns + s, off // hgn))

    def cws(off):
        return pl.BlockSpec((hgn, CONV_K, LANES), lambda b, s: (off // hgn, 0, 0))

    par = pl.BlockSpec((1, LANES), lambda b, s: (0, 0))
    return pl.pallas_call(
        _gdn_kernel,
        grid=(batch, ns),
        in_specs=[col(COL_GDN_Q), col(COL_GDN_K), col(COL_GDN_V), col(COL_GDN_Z),
                  pl.BlockSpec((n, LANES), lambda b, s: (b * ns + s, 0)),
                  cws(0), cws(GDN_HEADS), cws(2 * GDN_HEADS), par, par, par],
        out_specs=pl.BlockSpec((n, hgn * LANES), lambda b, s: (b * ns + s, 0)),
        out_shape=jax.ShapeDtypeStruct((t, BRANCH_WIDTH), BF16),
        scratch_shapes=[
            pltpu.VMEM((hgn, GDN_DK, GDN_DV), F32),
            pltpu.VMEM((3, SUBLANES, hgn * LANES), F32),
            pltpu.VMEM((hgn, n, GDN_DV + GDN_DK), BF16),
            pltpu.VMEM((hgn, n, LANES), F32),
            pltpu.VMEM((hgn, n, LANES), BF16),
            pltpu.VMEM((hgn, n, LANES), BF16),
            pltpu.VMEM((hgn, n, LANES), BF16),
            pltpu.VMEM((hgn, n, LANES), BF16),
            pltpu.VMEM((hgn, n, LANES), F32),
            pltpu.VMEM((hgn, 2 * CHUNK, LANES), BF16),
        ],
        compiler_params=_cparams(("parallel", "arbitrary")),
        name="gdn",
    )(proj, proj, proj, proj, small, cw, cw, cw, alog, dtb, gnorm.reshape(1, LANES))


GLA_SBLK = 256


def _gla_kernel(q_ref, k_ref, v_ref, r_ref, sm_ref, wg_ref, bg_ref, gn_ref, o_ref, state_ref):
    n = GLA_SBLK
    heads = range(GLA_HEADS)
    chunks = range(n // CHUNK)

    @pl.when(pl.program_id(1) == 0)
    def _():
        state_ref[...] = jnp.zeros_like(state_ref)

    pre = _dot(sm_ref[...].astype(BF16), wg_ref[...]) + bg_ref[...]
    log_g = _log_sigmoid(pre) / GLA_NORMALIZER
    gcum = _seg_cumsum_rows(log_g, CHUNK)
    row = lax.broadcasted_iota(jnp.int32, (CHUNK, CHUNK), 0)
    col = lax.broadcasted_iota(jnp.int32, (CHUNK, CHUNK), 1)
    causal = row >= col

    intra, update, q_dec, g_last = {}, {}, {}, {}
    for h in heads:
        kl = slice(h * GLA_DK, (h + 1) * GLA_DK)
        vl = slice(h * GLA_DV, (h + 1) * GLA_DV)
        g_h = gcum[:, kl]
        g_t = g_h.T
        q = q_ref[:, kl] * (GLA_DK ** -0.5)
        k = k_ref[:, kl]
        for c in chunks:
            sl = slice(c * CHUNK, (c + 1) * CHUNK)
            g_c = g_h[sl, :]
            ref = g_c[CHUNK // 2 - 1:CHUNK // 2, :]
            g_end = g_c[CHUNK - 1:CHUNK, :]
            q_c, k_c = q[sl, :], k[sl, :]
            v_c = v_ref[sl, vl].astype(BF16)
            a = _dot_nt((q_c * jnp.exp(g_c - ref)).astype(BF16), (k_c * jnp.exp(ref - g_c)).astype(BF16))
            intra[h, c] = _dot(jnp.where(causal, a, 0.0).astype(BF16), v_c)
            update[h, c] = _dot_tn((k_c * jnp.exp(g_end - g_c)).astype(BF16), v_c)
            q_dec[h, c] = (q_c * jnp.exp(g_c)).astype(BF16)
            g_last[h, c] = jnp.exp(g_t[:, (c + 1) * CHUNK - 1:(c + 1) * CHUNK])

    for h in heads:
        vl = slice(h * GLA_DV, (h + 1) * GLA_DV)
        state = state_ref[h]
        for c in chunks:
            sl = slice(c * CHUNK, (c + 1) * CHUNK)
            o = _dot(q_dec[h, c], state.astype(BF16)) + intra[h, c]
            state = state * g_last[h, c] + update[h, c]
            ms = jnp.mean(o * o, axis=-1, keepdims=True)
            o = o * lax.rsqrt(ms + EPS) * gn_ref[...]
            o_ref[sl, vl] = (o * _silu(r_ref[sl, vl])).astype(o_ref.dtype)
        state_ref[h] = state


def _gla(proj, small, w_gate, b_gate, gnorm, batch, seq):
    t = proj.shape[0]
    n = GLA_SBLK
    ns = seq // n
    kw, vw = GLA_HEADS * GLA_DK, GLA_HEADS * GLA_DV
    wg = jnp.zeros((LANES, kw), BF16).at[SM_G:SM_G + GLA_RANK, :].set(w_gate.astype(BF16))
    rows = lambda b, s: b * ns + s
    return pl.pallas_call(
        _gla_kernel,
        grid=(batch, ns),
        in_specs=[
            pl.BlockSpec((n, kw), lambda b, s: (rows(b, s), COL_GLA_Q * LANES // kw)),
            pl.BlockSpec((n, kw), lambda b, s: (rows(b, s), COL_GLA_K * LANES // kw)),
            pl.BlockSpec((n, vw), lambda b, s: (rows(b, s), COL_GLA_V * LANES // vw)),
            pl.BlockSpec((n, vw), lambda b, s: (rows(b, s), COL_GLA_R * LANES // vw)),
            pl.BlockSpec((n, LANES), lambda b, s: (rows(b, s), 0)),
            pl.BlockSpec((LANES, kw), lambda b, s: (0, 0)),
            pl.BlockSpec((1, kw), lambda b, s: (0, 0)),
            pl.BlockSpec((1, GLA_DV), lambda b, s: (0, 0)),
        ],
        out_specs=pl.BlockSpec((n, vw), lambda b, s: (rows(b, s), 0)),
        out_shape=jax.ShapeDtypeStruct((t, BRANCH_WIDTH), BF16),
        scratch_shapes=[pltpu.VMEM((GLA_HEADS, GLA_DK, GLA_DV), F32)],
        compiler_params=_cparams(("parallel", "arbitrary")),
        name="gla",
    )(proj, proj, proj, proj, small, wg, b_gate.reshape(1, -1).astype(F32), gnorm.reshape(1, GLA_DV))


def _split_hi_lo(x):
    hi = x.astype(BF16)
    lo = (x - hi.astype(F32)).astype(BF16)
    return hi, lo


SB_TILE = 256
SB_EXIT = -104.0
SB_HG = 4


def _sb_kernel(q_ref, k_ref, v_ref, qg_ref, kg_ref, o_ref, kn_ref, vb_ref):
    qi = pl.program_id(2)
    tl = SB_TILE

    heads = range(SB_HG)
    hl = [slice(h * SB_DH, (h + 1) * SB_DH) for h in heads]

    @pl.when(qi == 0)
    def _():
        for h in heads:
            kf = k_ref[:, hl[h]]
            ms = jnp.mean(kf * kf, axis=-1, keepdims=True)
            kn_ref[:, hl[h]] = (kf * lax.rsqrt(ms + EPS) * kg_ref[...]).astype(BF16)
        vb_ref[...] = v_ref[...].astype(BF16)

    qns = []
    for h in heads:
        q = q_ref[:, hl[h]]
        ms = jnp.mean(q * q, axis=-1, keepdims=True)
        qns.append((q * lax.rsqrt(ms + EPS) * qg_ref[...]).astype(BF16))
    row = lax.broadcasted_iota(jnp.int32, (tl, tl), 0)
    col = lax.broadcasted_iota(jnp.int32, (tl, tl), 1)
    strict = col < row
    after = (row > col).astype(BF16)
    scale = SB_DH ** -0.5

    def tile(j, carries, accs, diag):
        off = pl.multiple_of(j * tl, tl)
        zs = [_dot_nt(qns[h], kn_ref[pl.ds(off, tl), hl[h]]) * scale for h in heads]
        log_betas = [jnp.minimum(z, 0.0) - jnp.log(1.0 + jnp.exp(-jnp.abs(z))) for z in zs]
        log_1ms = [lb - z for lb, z in zip(log_betas, zs)]
        if diag:
            log_1ms = [jnp.where(strict, l, 0.0) for l in log_1ms]
        parts = [_split_hi_lo(l) for l in log_1ms]
        tails = [_dot(hi, after) + _dot(lo, after) + c for (hi, lo), c in zip(parts, carries)]
        attns = [jnp.exp(lb + t) for lb, t in zip(log_betas, tails)]
        if diag:
            attns = [jnp.where(strict, a, 0.0) for a in attns]
        accs = [accs[h] + _dot(attns[h].astype(BF16), vb_ref[pl.ds(off, tl), hl[h]]) for h in heads]
        carries = [c + jnp.sum(l, axis=-1, keepdims=True) for c, l in zip(carries, log_1ms)]
        return carries, accs

    def unfinished(carries):
        worst = carries[0]
        for c in carries[1:]:
            worst = jnp.maximum(worst, c)
        return jnp.max(worst) > SB_EXIT

    carries, accs = tile(qi, [jnp.zeros((tl, 1), F32)] * SB_HG, [jnp.zeros((tl, SB_DH), F32)] * SB_HG, True)

    def more(state):
        return jnp.logical_and(state[0] >= 0, state[1])

    def body(state):
        j, _, carries, accs = state
        carries, accs = tile(j, list(carries), list(accs), False)
        return j - 1, unfinished(carries), tuple(carries), tuple(accs)

    state = lax.while_loop(more, body, (qi - 1, unfinished(carries), tuple(carries), tuple(accs)))
    for h in heads:
        o_ref[:, hl[h]] = state[3][h].astype(o_ref.dtype)


def _sb(proj, q_gain, k_gain, batch, seq):
    t = proj.shape[0]
    nq = seq // SB_TILE
    hw = SB_HG * SB_DH
    return pl.pallas_call(
        _sb_kernel,
        grid=(batch, SB_HEADS // SB_HG, nq),
        in_specs=[
            pl.BlockSpec((SB_TILE, hw), lambda b, h, i: (b * nq + i, COL_SB_Q // SB_HG + h)),
            pl.BlockSpec((seq, hw), lambda b, h, i: (b, COL_SB_K // SB_HG + h)),
            pl.BlockSpec((seq, hw), lambda b, h, i: (b, COL_SB_V // SB_HG + h)),
            pl.BlockSpec((1, SB_DH), lambda b, h, i: (0, 0)),
            pl.BlockSpec((1, SB_DH), lambda b, h, i: (0, 0)),
        ],
        out_specs=pl.BlockSpec((SB_TILE, hw), lambda b, h, i: (b * nq + i, h)),
        out_shape=jax.ShapeDtypeStruct((t, BRANCH_WIDTH), BF16),
        scratch_shapes=[pltpu.VMEM((seq, hw), BF16), pltpu.VMEM((seq, hw), BF16)],
        compiler_params=_cparams(("parallel", "parallel", "arbitrary")),
        name="stickbreak",
    )(proj, proj, proj, q_gain.reshape(1, SB_DH), k_gain.reshape(1, SB_DH))


def _merge_kernel(ya_ref, yb_ref, yc_ref, wb_ref, ga_ref, gb_ref, gc_ref, o_ref):
    acc = _sigmoid(ga_ref[...]) * _dot(ya_ref[...], wb_ref[0])
    acc = acc + _sigmoid(gb_ref[...]) * _dot(yb_ref[...], wb_ref[1])
    acc = acc + _sigmoid(gc_ref[...]) * _dot(yc_ref[...], wb_ref[2])
    o_ref[...] = acc.astype(o_ref.dtype)


def _merge(ya, yb, yc, proj, w_branch, *, tm=1024, tn=512):
    t = ya.shape[0]
    tm = min(tm, t)
    nj = D_MODEL // tn
    g0 = COL_GATES * LANES // tn
    ysp = pl.BlockSpec((tm, BRANCH_WIDTH), lambda i, j: (i, 0))

    def gate(nb):
        return pl.BlockSpec((tm, tn), lambda i, j: (i, g0 + nb * nj + j))

    return pl.pallas_call(
        _merge_kernel,
        grid=(t // tm, nj),
        in_specs=[ysp, ysp, ysp,
                  pl.BlockSpec((N_BRANCH, BRANCH_WIDTH, tn), lambda i, j: (0, 0, j)),
                  gate(0), gate(1), gate(2)],
        out_specs=pl.BlockSpec((tm, tn), lambda i, j: (i, j)),
        out_shape=jax.ShapeDtypeStruct((t, D_MODEL), BF16),
        compiler_params=_cparams(("parallel", "arbitrary")),
        name="merge",
    )(ya, yb, yc, w_branch, proj, proj, proj)


def _out_proj_kernel(m_ref, w_ref, x_ref, g_ref, x1_ref, xn_ref):
    x1 = x_ref[...] + _dot(m_ref[...], w_ref[...])
    x1_ref[...] = x1
    ms = jnp.mean(x1 * x1, axis=-1, keepdims=True)
    xn_ref[...] = (x1 * lax.rsqrt(ms + EPS) * g_ref[...]).astype(BF16)


def _out_proj(merged, w_out, x2d, gain, *, tm=256):
    t, d = x2d.shape
    tm = min(tm, t)
    rows = pl.BlockSpec((tm, d), lambda i: (i, 0))
    return pl.pallas_call(
        _out_proj_kernel,
        grid=(t // tm,),
        in_specs=[rows, pl.BlockSpec((d, d), lambda i: (0, 0)), rows,
                  pl.BlockSpec((1, d), lambda i: (0, 0))],
        out_specs=[rows, rows],
        out_shape=[jax.ShapeDtypeStruct((t, d), F32), jax.ShapeDtypeStruct((t, d), BF16)],
        compiler_params=_cparams(("parallel",)),
        name="out_proj",
    )(merged, w_out, x2d, gain.reshape(1, d))


FFN_RB = 512
FFN_FC = 512
MOE_FC = 1024


def _ffn_kernel(be_ref, valid_ref, x_ref, w1_ref, w3_ref, w2_ref, *rest, residual, n_cast):
    rest = list(rest)
    res_ref = rest.pop(0) if residual else None
    cast_in = [rest.pop(0) for _ in range(n_cast)]
    o_ref = rest.pop(0)
    cast_out = [rest.pop(0) for _ in range(n_cast)]
    acc_ref, = rest
    i, c = pl.program_id(0), pl.program_id(1)

    for src, dst in zip(cast_in, cast_out):
        dst[...] = src[...].astype(dst.dtype)

    @pl.when(c == 0)
    def _():
        acc_ref[...] = jnp.zeros_like(acc_ref)

    @pl.when(valid_ref[i] == 1)
    def _():
        x = x_ref[...]
        h1 = _dot(x, w1_ref[...])
        h3 = _dot(x, w3_ref[...])
        acc_ref[...] += _dot((_silu(h1) * h3).astype(BF16), w2_ref[...])

    @pl.when(c == pl.num_programs(1) - 1)
    def _():
        y = acc_ref[...]
        if residual:
            y = y + res_ref[...]
        o_ref[...] = y.astype(o_ref.dtype)


def _cast_blocks(w, nblk, nc):
    w2d = w.reshape(-1, w.shape[-1])
    rows, cols = w2d.shape
    pack = 2 * SUBLANES
    if rows % nblk == 0 and cols % nc == 0 and (rows // nblk) % pack == 0 and (cols // nc) % LANES == 0:
        return w2d, (rows // nblk, cols // nc), lambda i, c, be, va: (i, c)
    if rows % (nblk * nc) == 0 and (rows // (nblk * nc)) % pack == 0:
        return w2d, (rows // (nblk * nc), cols), lambda i, c, be, va: (i * nc + c, 0)
    return None


def _ffn(x_rows, w1, w3, w2, block_expert, block_valid, residual=None, out_dtype=F32, fc=FFN_FC,
         cast=()):
    r, d = x_rows.shape
    f = w1.shape[-1]
    rb = min(FFN_RB, r)
    nblk, nc = r // rb, f // fc
    last = nc - 1

    def ccol(c, valid):
        return c * valid + last * (1 - valid)

    rows = pl.BlockSpec((rb, d), lambda i, c, be, va: (i, 0))
    in_specs = [
        rows,
        pl.BlockSpec((None, d, fc), lambda i, c, be, va: (be[i], 0, ccol(c, va[i]))),
        pl.BlockSpec((None, d, fc), lambda i, c, be, va: (be[i], 0, ccol(c, va[i]))),
        pl.BlockSpec((None, fc, d), lambda i, c, be, va: (be[i], ccol(c, va[i]), 0)),
    ]
    args = [x_rows, w1, w3, w2]
    if residual is not None:
        in_specs.append(rows)
        args.append(residual)
    out_specs = [rows]
    out_shape = [jax.ShapeDtypeStruct((r, d), out_dtype)]
    plans = [_cast_blocks(w, nblk, nc) for w in cast]
    for w2d, blk, imap in plans:
        in_specs.append(pl.BlockSpec(blk, imap))
        args.append(w2d)
        out_specs.append(pl.BlockSpec(blk, imap))
        out_shape.append(jax.ShapeDtypeStruct(w2d.shape, BF16))
    outs = pl.pallas_call(
        functools.partial(_ffn_kernel, residual=residual is not None, n_cast=len(plans)),
        grid_spec=pltpu.PrefetchScalarGridSpec(
            num_scalar_prefetch=2,
            grid=(nblk, nc),
            in_specs=in_specs,
            out_specs=out_specs,
            scratch_shapes=[pltpu.VMEM((rb, d), F32)],
        ),
        out_shape=out_shape,
        compiler_params=_cparams(("parallel", "arbitrary")),
        name="swiglu",
    )(block_expert, block_valid, *args)
    return outs[0], [o.reshape(w.shape) for o, w in zip(outs[1:], cast)]


ROUTE_TT = 512
RT_W, RT_RANK, RT_MEMBER = 0, 8, 16


def _router_kernel(x_ref, g_ref, wr_ref, rowtab_ref, coltab_ref, cum_ref, count_ref):
    i = pl.program_id(0)
    tt = x_ref.shape[0]

    @pl.when(i == 0)
    def _():
        count_ref[...] = jnp.zeros_like(count_ref)

    x = x_ref[...]
    ms = jnp.mean(x * x, axis=-1, keepdims=True)
    xn = x * lax.rsqrt(ms + EPS) * g_ref[...]
    logits = jnp.dot(xn, wr_ref[...], preferred_element_type=F32, precision=lax.Precision.HIGHEST)
    lg = logits.T[0:N_EXPERTS, :]
    eidx = lax.broadcasted_iota(jnp.int32, lg.shape, 0).astype(F32)
    m1 = jnp.max(lg, axis=0, keepdims=True)
    i1 = jnp.min(jnp.where(lg == m1, eidx, float(N_EXPERTS)), axis=0, keepdims=True)
    sel1 = eidx == i1
    lg2 = jnp.where(sel1, -jnp.inf, lg)
    m2 = jnp.max(lg2, axis=0, keepdims=True)
    i2 = jnp.min(jnp.where(lg2 == m2, eidx, float(N_EXPERTS)), axis=0, keepdims=True)
    sel2 = eidx == i2
    e2 = jnp.exp(m2 - m1)
    den = 1.0 + e2
    wts = jnp.where(sel1, 1.0 / den, 0.0) + jnp.where(sel2, e2 / den, 0.0)
    member = (sel1 | sel2).astype(F32)

    srow = lax.broadcasted_iota(jnp.int32, (tt, tt), 0)
    scol = lax.broadcasted_iota(jnp.int32, (tt, tt), 1)
    before = (srow < scol).astype(BF16)
    base = count_ref[...]
    rank = _dot(member.astype(BF16), before) + base
    cum_ref[...] = jnp.broadcast_to(base, (N_EXPERTS, LANES))[None]
    count_ref[...] = base + jnp.sum(member, axis=1, keepdims=True)

    rowtab = jnp.concatenate([wts, rank, member], axis=0)
    rowtab_ref[...] = rowtab
    pad = jnp.zeros((LANES - 3 * N_EXPERTS, tt), F32)
    coltab_ref[...] = jnp.concatenate([rowtab, pad], axis=0).T


def _router(x1, gain, w_router):
    t, d = x1.shape
    tt = min(ROUTE_TT, t)
    nt = t // tt
    wr = jnp.zeros((d, LANES), F32).at[:, :N_EXPERTS].set(w_router.astype(F32))
    return pl.pallas_call(
        _router_kernel,
        grid=(nt,),
        in_specs=[pl.BlockSpec((tt, d), lambda i: (i, 0)),
                  pl.BlockSpec((1, d), lambda i: (0, 0)),
                  pl.BlockSpec((d, LANES), lambda i: (0, 0))],
        out_specs=[pl.BlockSpec((3 * N_EXPERTS, tt), lambda i: (0, i)),
                   pl.BlockSpec((tt, LANES), lambda i: (i, 0)),
                   pl.BlockSpec((1, N_EXPERTS, LANES), lambda i: (i, 0, 0))],
        out_shape=[jax.ShapeDtypeStruct((3 * N_EXPERTS, t), F32),
                   jax.ShapeDtypeStruct((t, LANES), F32),
                   jax.ShapeDtypeStruct((nt, N_EXPERTS, LANES), F32)],
        scratch_shapes=[pltpu.VMEM((N_EXPERTS, 1), F32)],
        compiler_params=_cparams(("arbitrary",)),
        name="router",
    )(x1, gain.reshape(1, d), wr)


PAIR_WIN = 256


def _gather_kernel(pb_ref, pt_ref, pe_ref, plo_ref, pfirst_ref, pskip_ref, pwin_ref, x_ref, rt_ref,
                   o_ref):
    g = pl.program_id(0)
    rb, tt = o_ref.shape[0], x_ref.shape[0]
    win = pwin_ref[g]

    @pl.when(pfirst_ref[g] == 1)
    def _():
        o_ref[...] = jnp.zeros_like(o_ref)

    def target_row():
        e = pe_ref[g]
        rt = rt_ref[...]
        ridx = lax.broadcasted_iota(jnp.int32, rt.shape, 0)
        rank = jnp.sum(jnp.where(ridx == RT_RANK + e, rt, 0.0), axis=0, keepdims=True)
        member = jnp.sum(jnp.where(ridx == RT_MEMBER + e, rt, 0.0), axis=0, keepdims=True)
        return jnp.where(member > 0.5, rank - plo_ref[g].astype(F32), -1.0)

    @pl.when((pskip_ref[g] == 0) & (win >= 0))
    def _():
        w0 = pl.multiple_of(win, LANES)
        rows = (lax.broadcasted_iota(jnp.int32, (PAIR_WIN, tt), 0) + w0).astype(F32)
        onehot = (rows == target_row()).astype(BF16)
        o_ref[pl.ds(w0, PAIR_WIN), :] += _dot(onehot, x_ref[...]).astype(o_ref.dtype)

    @pl.when((pskip_ref[g] == 0) & (win < 0))
    def _():
        rows = lax.broadcasted_iota(jnp.int32, (rb, tt), 0).astype(F32)
        onehot = (rows == target_row()).astype(BF16)
        o_ref[...] += _dot(onehot, x_ref[...]).astype(o_ref.dtype)


def _gather_rows(xn, rowtab, pairs, nb):
    t, d = xn.shape
    tt = min(ROUTE_TT, t)
    rb = FFN_RB
    npairs = pairs[0].shape[0]
    return pl.pallas_call(
        _gather_kernel,
        grid_spec=pltpu.PrefetchScalarGridSpec(
            num_scalar_prefetch=7,
            grid=(npairs,),
            in_specs=[pl.BlockSpec((tt, d), lambda g, pb, pt, *_: (pt[g], 0)),
                      pl.BlockSpec((3 * N_EXPERTS, tt), lambda g, pb, pt, *_: (0, pt[g]))],
            out_specs=pl.BlockSpec((rb, d), lambda g, pb, *_: (pb[g], 0)),
        ),
        out_shape=jax.ShapeDtypeStruct((nb * rb, d), BF16),
        compiler_params=_cparams(("arbitrary",)),
        name="moe_gather",
    )(*pairs, xn, rowtab)


def _combine_kernel(pb_ref, pt_ref, pe_ref, plo_ref, pfirst_ref, pskip_ref, pwin_ref, y_ref, ct_ref,
                    x_ref, o_ref):
    g = pl.program_id(0)
    tt, rb = o_ref.shape[0], y_ref.shape[0]
    win = pwin_ref[g]

    @pl.when(pfirst_ref[g] == 1)
    def _():
        o_ref[...] = x_ref[...]

    def weight_and_target():
        e = pe_ref[g]
        ct = ct_ref[...]
        lane = lax.broadcasted_iota(jnp.int32, ct.shape, 1)
        wcol = jnp.sum(jnp.where(lane == RT_W + e, ct, 0.0), axis=1, keepdims=True)
        rank = jnp.sum(jnp.where(lane == RT_RANK + e, ct, 0.0), axis=1, keepdims=True)
        member = jnp.sum(jnp.where(lane == RT_MEMBER + e, ct, 0.0), axis=1, keepdims=True)
        return wcol, jnp.where(member > 0.5, rank - plo_ref[g].astype(F32), -1.0)

    @pl.when((pskip_ref[g] == 0) & (win >= 0))
    def _():
        w0 = pl.multiple_of(win, LANES)
        wcol, target = weight_and_target()
        cols = (lax.broadcasted_iota(jnp.int32, (tt, PAIR_WIN), 1) + w0).astype(F32)
        onehot = (cols == target).astype(BF16)
        o_ref[...] += wcol * _dot(onehot, y_ref[pl.ds(w0, PAIR_WIN), :])

    @pl.when((pskip_ref[g] == 0) & (win < 0))
    def _():
        wcol, target = weight_and_target()
        cols = lax.broadcasted_iota(jnp.int32, (tt, rb), 1).astype(F32)
        onehot = (cols == target).astype(BF16)
        o_ref[...] += wcol * _dot(onehot, y_ref[...])


def _combine(y_rows, coltab, x1, pairs):
    t, d = x1.shape
    tt = min(ROUTE_TT, t)
    rb = FFN_RB
    npairs = pairs[0].shape[0]
    return pl.pallas_call(
        _combine_kernel,
        grid_spec=pltpu.PrefetchScalarGridSpec(
            num_scalar_prefetch=7,
            grid=(npairs,),
            in_specs=[pl.BlockSpec((rb, d), lambda g, pb, *_: (pb[g], 0)),
                      pl.BlockSpec((tt, LANES), lambda g, pb, pt, *_: (pt[g], 0)),
                      pl.BlockSpec((tt, d), lambda g, pb, pt, *_: (pt[g], 0))],
            out_specs=pl.BlockSpec((tt, d), lambda g, pb, pt, *_: (pt[g], 0)),
        ),
        out_shape=jax.ShapeDtypeStruct((t, d), F32),
        compiler_params=_cparams(("arbitrary",)),
        name="moe_combine",
    )(*pairs, y_rows, coltab, x1)


def _pair_tables(cum, counts, nb, order):
    nt = cum.shape[0]
    rb = FFN_RB
    npairs = nb + N_EXPERTS * nt
    padded = (counts + rb - 1) // rb * rb
    seg_end = jnp.cumsum(padded)
    seg_start = seg_end - padded
    blk = jnp.arange(nb, dtype=jnp.int32)
    blk_row = blk * rb
    blk_e = jnp.minimum(jnp.searchsorted(seg_end, blk_row, side='right'), N_EXPERTS - 1).astype(jnp.int32)
    blk_lo = blk_row - seg_start[blk_e]
    blk_valid = (blk_row < seg_end[-1]) & (blk_lo < counts[blk_e])
    blk_hi = jnp.minimum(blk_lo + rb, counts[blk_e])
    tile_lo = cum[:, blk_e]
    tile_hi = jnp.concatenate([cum[1:], counts[None, :]], axis=0)[:, blk_e]
    overlap = blk_valid[None, :] & (tile_lo < blk_hi[None, :]) & (tile_hi > blk_lo[None, :]) \
        & (tile_hi > tile_lo)
    if order == 'block':
        flat = overlap.T.reshape(-1)
        idx = jnp.nonzero(flat, size=npairs, fill_value=-1)[0].astype(jnp.int32)
        n_valid = jnp.sum(flat.astype(jnp.int32))
        last = idx[jnp.maximum(n_valid - 1, 0)]
        idx = jnp.where(idx < 0, last, idx)
        pb, pt = idx // nt, idx % nt
        major = pb
    else:
        flat = overlap.reshape(-1)
        idx = jnp.nonzero(flat, size=npairs, fill_value=-1)[0].astype(jnp.int32)
        n_valid = jnp.sum(flat.astype(jnp.int32))
        last = idx[jnp.maximum(n_valid - 1, 0)]
        idx = jnp.where(idx < 0, last, idx)
        pt, pb = idx // nb, idx % nb
        major = pt
    pos = jnp.arange(npairs, dtype=jnp.int32)
    skip = (pos >= n_valid).astype(jnp.int32)
    first = jnp.concatenate([jnp.ones((1,), jnp.int32),
                             (major[1:] != major[:-1]).astype(jnp.int32)]) * (1 - skip)
    r0 = jnp.maximum(tile_lo[pt, pb] - blk_lo[pb], 0)
    r1 = jnp.minimum(tile_hi[pt, pb], blk_hi[pb]) - blk_lo[pb]
    w0 = jnp.minimum(r0 // LANES * LANES, rb - PAIR_WIN)
    win = jnp.where(r1 <= w0 + PAIR_WIN, w0, -1)
    pairs = (pb.astype(jnp.int32), pt.astype(jnp.int32), blk_e[pb], blk_lo[pb].astype(jnp.int32),
             first.astype(jnp.int32), skip, win.astype(jnp.int32))
    return pairs, blk_e, blk_valid.astype(jnp.int32)


def _moe(x1, xn, gain, w_router, w1, w3, w2):
    t, d = x1.shape
    rowtab, coltab, cum3 = _router(x1, gain, w_router)
    cum = cum3[:, :, 0].astype(jnp.int32)
    tt = min(ROUTE_TT, t)
    last_members = jnp.sum(rowtab[RT_MEMBER:RT_MEMBER + N_EXPERTS, t - tt:], axis=1).astype(jnp.int32)
    counts = cum[-1] + last_members
    nb = (t * TOP_K) // FFN_RB + N_EXPERTS
    pairs_g, blk_e, blk_valid = _pair_tables(cum, counts, nb, 'block')
    pairs_c, _, _ = _pair_tables(cum, counts, nb, 'tile')
    x_rows = _gather_rows(xn, rowtab, pairs_g, nb)
    y_rows, _ = _ffn(x_rows, w1, w3, w2, blk_e, blk_valid, out_dtype=BF16, fc=MOE_FC)
    return _combine(y_rows, coltab, x1, pairs_c)


def _split_w_in(w):
    main = jnp.concatenate([w[:, 0:4096], w[:, 4112:7184], w[:, 7200:16416]], axis=1).astype(BF16)
    small = jnp.concatenate([w[:, 4096:4112], w[:, 7184:7200],
                             jnp.zeros((w.shape[0], LANES - 32), w.dtype)], axis=1).astype(BF16)
    return main, small


def kernel(x, norm_mix, w_in, conv_w, gdn_a_log, gdn_dt_bias, gdn_norm, gla_w_gate, gla_b_gate,
           gla_norm, sb_q_norm, sb_k_norm, w_branch, w_out, norm_ffn, ffn_w1, ffn_w3, ffn_w2,
           moe_router, moe_w1, moe_w3, moe_w2):
    batch, seq, d = x.shape
    t = batch * seq
    depth = w_in.shape[0]
    x2 = x.reshape(t, d)
    expert_bf16 = None
    for layer in range(depth):
        w_main, w_small = _split_w_in(w_in[layer])
        proj, small = _in_proj(x2, norm_mix[layer], w_main, w_small)
        ya = _gdn(proj, small, conv_w[layer], gdn_a_log[layer], gdn_dt_bias[layer], gdn_norm[layer],
                  batch, seq)
        yb = _gla(proj, small, gla_w_gate[layer], gla_b_gate[layer], gla_norm[layer], batch, seq)
        yc = _sb(proj, sb_q_norm[layer], sb_k_norm[layer], batch, seq)
        merged = _merge(ya, yb, yc, proj, w_branch[layer].astype(BF16))
        x1, xn = _out_proj(merged, w_out[layer].astype(BF16), x2, norm_ffn[layer])
        i = layer // 2
        if layer % 2 == 0:
            nblk = t // min(FFN_RB, t)
            cast = ()
            if layer + 1 < depth:
                nxt = (moe_w1[i], moe_w3[i], moe_w2[i])
                if all(_cast_blocks(w, nblk, D_FF // FFN_FC) is not None for w in nxt):
                    cast = nxt
            x2, casted = _ffn(xn, ffn_w1[i][None].astype(BF16), ffn_w3[i][None].astype(BF16),
                              ffn_w2[i][None].astype(BF16), jnp.zeros((nblk,), jnp.int32),
                              jnp.ones((nblk,), jnp.int32), residual=x1, cast=cast)
            expert_bf16 = tuple(casted) if cast else None
        else:
            if expert_bf16 is None:
                expert_bf16 = (moe_w1[i].astype(BF16), moe_w3[i].astype(BF16), moe_w2[i].astype(BF16))
            x2 = _moe(x1, xn, norm_ffn[layer], moe_router[i], *expert_bf16)
            expert_bf16 = None
    return x2.reshape(batch, seq, d)
```

```python
import functools

import jax
import jax.numpy as jnp
from jax import lax
from jax.experimental import pallas as pl
from jax.experimental.pallas import tpu as pltpu

F32 = jnp.float32
BF16 = jnp.bfloat16

D_MODEL = 2048
GDN_HEADS, GDN_DK, GDN_DV, CONV_K, CHUNK = 8, 128, 128, 4, 64
GLA_HEADS, GLA_DK, GLA_DV, GLA_RANK, GLA_NORMALIZER = 4, 128, 256, 16, 16.0
SB_HEADS, SB_DH, SB_BLOCK = 8, 128, 128
N_BRANCH, BRANCH_WIDTH = 3, 1024
D_FF, N_EXPERTS, TOP_K = 7168, 8, 2
EPS = 1e-6

LANES = 128
SUBLANES = 8
VMEM_LIMIT = 48 * 1024 * 1024

COL_GDN_Q, COL_GDN_K, COL_GDN_V, COL_GDN_Z = 0, 8, 16, 24
COL_GLA_Q, COL_GLA_K, COL_GLA_V, COL_GLA_R = 32, 36, 40, 48
COL_SB_Q, COL_SB_K, COL_SB_V = 56, 64, 72
COL_GATES = 80
N_MAIN = 128 * LANES
SM_A, SM_B, SM_G = 0, 8, 16


def _cparams(sem):
    return pltpu.CompilerParams(dimension_semantics=sem, vmem_limit_bytes=VMEM_LIMIT)


def _dot(a, b):
    return jnp.dot(a, b, preferred_element_type=F32)


def _dot_nt(a, b):
    return lax.dot_general(a, b, (((1,), (1,)), ((), ())), preferred_element_type=F32)


def _dot_tn(a, b):
    return lax.dot_general(a, b, (((0,), (0,)), ((), ())), preferred_element_type=F32)


def _sigmoid(x):
    return 1.0 / (1.0 + jnp.exp(-x))


def _silu(x):
    return x * _sigmoid(x)


def _softplus(x):
    return jnp.maximum(x, 0.0) + jnp.log1p(jnp.exp(-jnp.abs(x)))


def _log_sigmoid(x):
    return -_softplus(-x)


def _seg_cumsum_rows(x, seg):
    pos = lax.broadcasted_iota(jnp.int32, x.shape, 0) & (seg - 1)
    s = 1
    while s < seg:
        x = x + jnp.where(pos >= s, pltpu.roll(x, s, axis=0), 0.0)
        s *= 2
    return x


def _cast_blocks(w, n0, n1):
    w2d = w.reshape(-1, w.shape[-1])
    rows, cols = w2d.shape
    pack = 2 * SUBLANES
    if rows % (n0 * n1) == 0 and (rows // (n0 * n1)) % pack == 0:
        return w2d, (rows // (n0 * n1), cols), lambda i, j, *_: (i * n1 + j, 0)
    if rows % n0 == 0 and cols % n1 == 0 and (rows // n0) % pack == 0 and (cols // n1) % LANES == 0:
        return w2d, (rows // n0, cols // n1), lambda i, j, *_: (i, j)
    return None


def _cast_specs(cast, n0, n1):
    plans = [_cast_blocks(w, n0, n1) for w in cast]
    specs = [pl.BlockSpec(blk, imap) for _, blk, imap in plans]
    shapes = [jax.ShapeDtypeStruct(w2d.shape, BF16) for w2d, _, _ in plans]
    return [w2d for w2d, _, _ in plans], specs, shapes


def _in_proj_kernel(x_ref, g_ref, wm_ref, ws_ref, *rest, n_cast):
    cast_in, rest = rest[:n_cast], rest[n_cast:]
    proj_ref, small_ref = rest[:2]
    cast_out, (xn_ref,) = rest[2:2 + n_cast], rest[2 + n_cast:]
    for src, dst in zip(cast_in, cast_out):
        dst[...] = src[...].astype(dst.dtype)

    @pl.when(pl.program_id(1) == 0)
    def _():
        x = x_ref[...]
        ms = jnp.mean(x * x, axis=-1, keepdims=True)
        xn = (x * lax.rsqrt(ms + EPS) * g_ref[...]).astype(BF16)
        xn_ref[...] = xn
        small_ref[...] = _dot(xn, ws_ref[...])

    proj_ref[...] = _dot(xn_ref[...], wm_ref[...])


IN_PROJ_TM, IN_PROJ_TN = 1024, 1024


def _in_proj_grid(t):
    return t // min(IN_PROJ_TM, t), N_MAIN // IN_PROJ_TN


def _in_proj(x2d, gain, w_main, w_small, cast=()):
    t, d = x2d.shape
    tm, tn = min(IN_PROJ_TM, t), IN_PROJ_TN
    n = w_main.shape[1]
    grid = _in_proj_grid(t)
    cast_args, cast_specs, cast_shapes = _cast_specs(cast, *grid)
    outs = pl.pallas_call(
        functools.partial(_in_proj_kernel, n_cast=len(cast)),
        grid=grid,
        in_specs=[
            pl.BlockSpec((tm, d), lambda i, j: (i, 0)),
            pl.BlockSpec((1, d), lambda i, j: (0, 0)),
            pl.BlockSpec((d, tn), lambda i, j: (0, j)),
            pl.BlockSpec((d, LANES), lambda i, j: (0, 0)),
        ] + cast_specs,
        out_specs=[
            pl.BlockSpec((tm, tn), lambda i, j: (i, j)),
            pl.BlockSpec((tm, LANES), lambda i, j: (i, 0)),
        ] + cast_specs,
        out_shape=[jax.ShapeDtypeStruct((t, n), F32), jax.ShapeDtypeStruct((t, LANES), F32)] + cast_shapes,
        scratch_shapes=[pltpu.VMEM((tm, d), BF16)],
        compiler_params=_cparams(("parallel", "arbitrary")),
        name="in_proj",
    )(x2d, gain.reshape(1, d), w_main, w_small, *cast_args)
    return outs[0], outs[1], [o.reshape(w.shape) for o, w in zip(outs[2:], cast)]


GDN_SBLK = 256


def _causal_conv_silu(x, prev, w):
    n = x.shape[0]
    row8 = lax.broadcasted_iota(jnp.int32, (SUBLANES, LANES), 0)
    acc = x * w[CONV_K - 1:CONV_K, :]
    for j in range(1, CONV_K):
        xs = pltpu.roll(x, j, axis=0)
        ps = pltpu.roll(prev, j, axis=0)
        top = jnp.where(row8 < j, ps, xs[0:SUBLANES, :])
        xs = jnp.concatenate([top, xs[SUBLANES:n, :]], axis=0)
        acc = acc + xs * w[CONV_K - 1 - j:CONV_K - j, :]
    return _silu(acc)


def _unit_lower_inverses(l_mats, row, col):
    eye = (row == col).astype(F32)
    blk16 = (row >> 4) == (col >> 4)
    blk32 = (row >> 5) == (col >> 5)
    bf = lambda ms: [m.astype(BF16) for m in ms]
    mm = lambda xs, ys: [_dot(a, b) for a, b in zip(xs, ys)]

    d32 = [jnp.where(blk16, l, 0.0) for l in l_mats]
    d = bf(d32)
    e1 = bf([jnp.where(blk32 & (~blk16), l, 0.0) for l in l_mats])
    e2 = bf([jnp.where(blk32, 0.0, l) for l in l_mats])
    d2 = bf(mm(d, d))
    d4 = bf(mm(d2, d2))
    d8 = bf(mm(d4, d4))
    x = [eye - m for m in d32]
    for power in (d2, d4, d8):
        x = [a + b for a, b in zip(x, mm(bf(x), power))]
    for e in (e1, e2):
        xb = bf(x)
        x = [a - b for a, b in zip(x, mm(bf(mm(xb, e)), xb))]
    return x


GDN_HG = GDN_HEADS


def _gdn_kernel(q_ref, k_ref, v_ref, z_ref, sm_ref, wq_ref, wk_ref, wv_ref, alog_ref, dtb_ref,
                gn_ref, o_ref, state_ref, tail_ref, rhs_ref, u_ref, w_ref, qk_ref, qd_ref, kd_ref,
                gl_ref, vn_ref):
    n = GDN_SBLK
    pair = 2 * CHUNK

    @pl.when(pl.program_id(1) == 0)
    def _():
        state_ref[...] = jnp.zeros_like(state_ref)
        tail_ref[...] = jnp.zeros_like(tail_ref)

    sm = sm_ref[...]
    g_all = _seg_cumsum_rows(-jnp.exp(alog_ref[...]) * _softplus(sm + dtb_ref[...]), CHUNK)
    eg_all = jnp.exp(g_all)
    beta_all = _sigmoid(sm)
    row = lax.broadcasted_iota(jnp.int32, (pair, pair), 0)
    col = lax.broadcasted_iota(jnp.int32, (pair, pair), 1)
    same = (row >> 6) == (col >> 6)
    causal = same & (row >= col)
    strict = same & (row > col)
    first = lax.broadcasted_iota(jnp.int32, (pair, LANES), 0) < CHUNK

    l_mats = []
    for j in range(GDN_HG):
        hl = slice(j * LANES, (j + 1) * LANES)

        def conv(x_ref, w_ref_, idx):
            x = x_ref[:, hl]
            y = _causal_conv_silu(x, tail_ref[idx, :, hl], w_ref_[j])
            tail_ref[idx, :, hl] = x[n - SUBLANES:n, :]
            return y

        q = conv(q_ref, wq_ref, 0)
        k = conv(k_ref, wk_ref, 1)
        v = conv(v_ref, wv_ref, 2)
        q = q * lax.rsqrt(jnp.sum(q * q, axis=-1, keepdims=True) + EPS) * (GDN_DK ** -0.5)
        k = k * lax.rsqrt(jnp.sum(k * k, axis=-1, keepdims=True) + EPS)
        beta = beta_all[:, SM_B + j:SM_B + j + 1]
        eg = eg_all[:, SM_A + j:SM_A + j + 1]
        g = jnp.broadcast_to(g_all[:, SM_A + j:SM_A + j + 1], (n, LANES))
        kb = k * beta
        vb = v * beta

        rhs_ref[j, :, 0:GDN_DV] = vb.astype(BF16)
        rhs_ref[j, :, GDN_DV:] = (kb * eg).astype(BF16)
        qd_ref[j] = (q * eg).astype(BF16)
        for p in range(n // pair):
            sl = slice(p * pair, (p + 1) * pair)
            g_p = g[sl, :]
            g_rowmat = g_p.T
            decay = jnp.exp(jnp.where(causal, g_p - g_rowmat, -jnp.inf))
            k_p = k[sl, :].astype(BF16)
            kk = _dot_nt(kb[sl, :].astype(BF16), k_p)
            l_mats.append(jnp.where(strict, kk * decay, 0.0))
            qk = _dot_nt(q[sl, :].astype(BF16), k_p) * decay
            qk_ref[j, sl, :] = jnp.where(causal, qk, 0.0).astype(BF16)
            g_last = jnp.where(first, g_p[CHUNK - 1:CHUNK, :], g_p[pair - 1:pair, :])
            kd_ref[j, sl, :] = (k[sl, :] * jnp.exp(g_last - g_p)).astype(BF16)
            gl_ref[j, sl, :] = jnp.exp(g_last)

    x_invs = _unit_lower_inverses(l_mats, row, col)
    for idx, x_inv in enumerate(x_invs):
        j, p = divmod(idx, n // pair)
        sl = slice(p * pair, (p + 1) * pair)
        sol = _dot(x_inv.astype(BF16), rhs_ref[j, sl, :])
        u_ref[j, sl, :] = sol[:, 0:GDN_DV]
        w_ref[j, sl, :] = sol[:, GDN_DV:].astype(BF16)

    heads = range(GDN_HG)
    states = [state_ref[j] for j in heads]
    for c in range(n // CHUNK):
        sl = slice(c * CHUNK, (c + 1) * CHUNK)
        half = c % 2
        if half == 0:
            vn_ref[...] = jnp.zeros_like(vn_ref)
        sbs = [s.astype(BF16) for s in states]
        v_news = [u_ref[j, sl, :] - _dot(w_ref[j, sl, :], sbs[j]) for j in heads]
        vnbs = [v.astype(BF16) for v in v_news]
        for j in heads:
            vn_ref[j, half * CHUNK:(half + 1) * CHUNK, :] = vnbs[j]
        outs = [_dot(qd_ref[j, sl, :], sbs[j]) + _dot(qk_ref[j, sl, :], vn_ref[j]) for j in heads]
        states = [states[j] * gl_ref[j, c * CHUNK:c * CHUNK + 1, :] + _dot_tn(kd_ref[j, sl, :], vnbs[j])
                  for j in heads]
        for j in heads:
            hl = slice(j * LANES, (j + 1) * LANES)
            o = outs[j]
            ms = jnp.mean(o * o, axis=-1, keepdims=True)
            o = o * lax.rsqrt(ms + EPS) * gn_ref[...]
            o_ref[sl, hl] = (o * _silu(z_ref[sl, hl])).astype(o_ref.dtype)
    for j in heads:
        state_ref[j] = states[j]


def _gdn(proj, small, conv_w, a_log, dt_bias, gnorm, batch, seq):
    t = proj.shape[0]
    n = GDN_SBLK
    hgn = GDN_HG
    ns = seq // n
    cw = conv_w.reshape(CONV_K, 3 * GDN_HEADS, LANES).transpose(1, 0, 2)
    lane_pad = (SM_A, LANES - SM_A - GDN_HEADS)
    alog = jnp.pad(a_log.astype(F32), lane_pad).reshape(1, LANES)
    dtb = jnp.pad(dt_bias.astype(F32), lane_pad).reshape(1, LANES)

    def col(off):
        return pl.BlockSpec((n, hgn * LANES), lambda b, s: (b * ns + s, off // hgn))

    def cws(off):
        return pl.BlockSpec((hgn, CONV_K, LANES), lambda b, s: (off // hgn, 0, 0))

    par = pl.BlockSpec((1, LANES), lambda b, s: (0, 0))
    return pl.pallas_call(
        _gdn_kernel,
        grid=(batch, ns),
        in_specs=[col(COL_GDN_Q), col(COL_GDN_K), col(COL_GDN_V), col(COL_GDN_Z),
                  pl.BlockSpec((n, LANES), lambda b, s: (b * ns + s, 0)),
                  cws(0), cws(GDN_HEADS), cws(2 * GDN_HEADS), par, par, par],
        out_specs=pl.BlockSpec((n, hgn * LANES), lambda b, s: (b * ns + s, 0)),
        out_shape=jax.ShapeDtypeStruct((t, BRANCH_WIDTH), BF16),
        scratch_shapes=[
            pltpu.VMEM((hgn, GDN_DK, GDN_DV), F32),
            pltpu.VMEM((3, SUBLANES, hgn * LANES), F32),
            pltpu.VMEM((hgn, n, GDN_DV + GDN_DK), BF16),
            pltpu.VMEM((hgn, n, LANES), F32),
            pltpu.VMEM((hgn, n, LANES), BF16),
            pltpu.VMEM((hgn, n, LANES), BF16),
            pltpu.VMEM((hgn, n, LANES), BF16),
            pltpu.VMEM((hgn, n, LANES), BF16),
            pltpu.VMEM((hgn, n, LANES), F32),
            pltpu.VMEM((hgn, 2 * CHUNK, LANES), BF16),
        ],
        compiler_params=_cparams(("parallel", "arbitrary")),
        name="gdn",
    )(proj, proj, proj, proj, small, cw, cw, cw, alog, dtb, gnorm.reshape(1, LANES))


GLA_SBLK = 256


def _gla_kernel(q_ref, k_ref, v_ref, r_ref, sm_ref, wg_ref, bg_ref, gn_ref, o_ref, state_ref):
    n = GLA_SBLK
    heads = range(GLA_HEADS)
    chunks = range(n // CHUNK)

    @pl.when(pl.program_id(1) == 0)
    def _():
        state_ref[...] = jnp.zeros_like(state_ref)

    pre = _dot(sm_ref[...].astype(BF16), wg_ref[...]) + bg_ref[...]
    log_g = _log_sigmoid(pre) / GLA_NORMALIZER
    gcum = _seg_cumsum_rows(log_g, CHUNK)
    row = lax.broadcasted_iota(jnp.int32, (CHUNK, CHUNK), 0)
    col = lax.broadcasted_iota(jnp.int32, (CHUNK, CHUNK), 1)
    causal = row >= col

    intra, update, q_dec, g_last = {}, {}, {}, {}
    for h in heads:
        kl = slice(h * GLA_DK, (h + 1) * GLA_DK)
        vl = slice(h * GLA_DV, (h + 1) * GLA_DV)
        g_h = gcum[:, kl]
        g_t = g_h.T
        q = q_ref[:, kl] * (GLA_DK ** -0.5)
        k = k_ref[:, kl]
        for c in chunks:
            sl = slice(c * CHUNK, (c + 1) * CHUNK)
            g_c = g_h[sl, :]
            ref = g_c[CHUNK // 2 - 1:CHUNK // 2, :]
            g_end = g_c[CHUNK - 1:CHUNK, :]
            q_c, k_c = q[sl, :], k[sl, :]
            v_c = v_ref[sl, vl].astype(BF16)
            a = _dot_nt((q_c * jnp.exp(g_c - ref)).astype(BF16), (k_c * jnp.exp(ref - g_c)).astype(BF16))
            intra[h, c] = _dot(jnp.where(causal, a, 0.0).astype(BF16), v_c)
            update[h, c] = _dot_tn((k_c * jnp.exp(g_end - g_c)).astype(BF16), v_c)
            q_dec[h, c] = (q_c * jnp.exp(g_c)).astype(BF16)
            g_last[h, c] = jnp.exp(g_t[:, (c + 1) * CHUNK - 1:(c + 1) * CHUNK])

    for h in heads:
        vl = slice(h * GLA_DV, (h + 1) * GLA_DV)
        state = state_ref[h]
        for c in chunks:
            sl = slice(c * CHUNK, (c + 1) * CHUNK)
            o = _dot(q_dec[h, c], state.astype(BF16)) + intra[h, c]
            state = state * g_last[h, c] + update[h, c]
            ms = jnp.mean(o * o, axis=-1, keepdims=True)
            o = o * lax.rsqrt(ms + EPS) * gn_ref[...]
            o_ref[sl, vl] = (o * _silu(r_ref[sl, vl])).astype(o_ref.dtype)
        state_ref[h] = state


def _gla(proj, small, w_gate, b_gate, gnorm, batch, seq):
    t = proj.shape[0]
    n = GLA_SBLK
    ns = seq // n
    kw, vw = GLA_HEADS * GLA_DK, GLA_HEADS * GLA_DV
    wg = jnp.zeros((LANES, kw), BF16).at[SM_G:SM_G + GLA_RANK, :].set(w_gate.astype(BF16))
    rows = lambda b, s: b * ns + s
    return pl.pallas_call(
        _gla_kernel,
        grid=(batch, ns),
        in_specs=[
            pl.BlockSpec((n, kw), lambda b, s: (rows(b, s), COL_GLA_Q * LANES // kw)),
            pl.BlockSpec((n, kw), lambda b, s: (rows(b, s), COL_GLA_K * LANES // kw)),
            pl.BlockSpec((n, vw), lambda b, s: (rows(b, s), COL_GLA_V * LANES // vw)),
            pl.BlockSpec((n, vw), lambda b, s: (rows(b, s), COL_GLA_R * LANES // vw)),
            pl.BlockSpec((n, LANES), lambda b, s: (rows(b, s), 0)),
            pl.BlockSpec((LANES, kw), lambda b, s: (0, 0)),
            pl.BlockSpec((1, kw), lambda b, s: (0, 0)),
            pl.BlockSpec((1, GLA_DV), lambda b, s: (0, 0)),
        ],
        out_specs=pl.BlockSpec((n, vw), lambda b, s: (rows(b, s), 0)),
        out_shape=jax.ShapeDtypeStruct((t, BRANCH_WIDTH), BF16),
        scratch_shapes=[pltpu.VMEM((GLA_HEADS, GLA_DK, GLA_DV), F32)],
        compiler_params=_cparams(("parallel", "arbitrary")),
        name="gla",
    )(proj, proj, proj, proj, small, wg, b_gate.reshape(1, -1).astype(F32), gnorm.reshape(1, GLA_DV))


def _split_hi_lo(x):
    hi = x.astype(BF16)
    lo = (x - hi.astype(F32)).astype(BF16)
    return hi, lo


SB_TILE = 256
SB_EXIT = -104.0
SB_HG = 4


def _sb_kernel(q_ref, k_ref, v_ref, qg_ref, kg_ref, o_ref, kn_ref, vb_ref):
    qi = pl.program_id(2)
    tl = SB_TILE

    heads = range(SB_HG)
    hl = [slice(h * SB_DH, (h + 1) * SB_DH) for h in heads]

    @pl.when(qi == 0)
    def _():
        for h in heads:
            kf = k_ref[:, hl[h]]
            ms = jnp.mean(kf * kf, axis=-1, keepdims=True)
            kn_ref[:, hl[h]] = (kf * lax.rsqrt(ms + EPS) * kg_ref[...]).astype(BF16)
        vb_ref[...] = v_ref[...].astype(BF16)

    qns = []
    for h in heads:
        q = q_ref[:, hl[h]]
        ms = jnp.mean(q * q, axis=-1, keepdims=True)
        qns.append((q * lax.rsqrt(ms + EPS) * qg_ref[...]).astype(BF16))
    row = lax.broadcasted_iota(jnp.int32, (tl, tl), 0)
    col = lax.broadcasted_iota(jnp.int32, (tl, tl), 1)
    strict = col < row
    after = (row > col).astype(BF16)
    scale = SB_DH ** -0.5

    def tile(j, carries, accs, diag):
        off = pl.multiple_of(j * tl, tl)
        zs = [_dot_nt(qns[h], kn_ref[pl.ds(off, tl), hl[h]]) * scale for h in heads]
        log_betas = [jnp.minimum(z, 0.0) - jnp.log(1.0 + jnp.exp(-jnp.abs(z))) for z in zs]
        log_1ms = [lb - z for lb, z in zip(log_betas, zs)]
        if diag:
            log_1ms = [jnp.where(strict, l, 0.0) for l in log_1ms]
        parts = [_split_hi_lo(l) for l in log_1ms]
        tails = [_dot(hi, after) + _dot(lo, after) + c for (hi, lo), c in zip(parts, carries)]
        attns = [jnp.exp(lb + t) for lb, t in zip(log_betas, tails)]
        if diag:
            attns = [jnp.where(strict, a, 0.0) for a in attns]
        accs = [accs[h] + _dot(attns[h].astype(BF16), vb_ref[pl.ds(off, tl), hl[h]]) for h in heads]
        carries = [c + jnp.sum(l, axis=-1, keepdims=True) for c, l in zip(carries, log_1ms)]
        return carries, accs

    def unfinished(carries):
        worst = carries[0]
        for c in carries[1:]:
            worst = jnp.maximum(worst, c)
        return jnp.max(worst) > SB_EXIT

    carries, accs = tile(qi, [jnp.zeros((tl, 1), F32)] * SB_HG, [jnp.zeros((tl, SB_DH), F32)] * SB_HG, True)

    def more(state):
        return jnp.logical_and(state[0] >= 0, state[1])

    def body(state):
        j, _, carries, accs = state
        carries, accs = tile(j, list(carries), list(accs), False)
        return j - 1, unfinished(carries), tuple(carries), tuple(accs)

    state = lax.while_loop(more, body, (qi - 1, unfinished(carries), tuple(carries), tuple(accs)))
    for h in heads:
        o_ref[:, hl[h]] = state[3][h].astype(o_ref.dtype)


def _sb(proj, q_gain, k_gain, batch, seq):
    t = proj.shape[0]
    nq = seq // SB_TILE
    hw = SB_HG * SB_DH
    return pl.pallas_call(
        _sb_kernel,
        grid=(batch, SB_HEADS // SB_HG, nq),
        in_specs=[
            pl.BlockSpec((SB_TILE, hw), lambda b, h, i: (b * nq + i, COL_SB_Q // SB_HG + h)),
            pl.BlockSpec((seq, hw), lambda b, h, i: (b, COL_SB_K // SB_HG + h)),
            pl.BlockSpec((seq, hw), lambda b, h, i: (b, COL_SB_V // SB_HG + h)),
            pl.BlockSpec((1, SB_DH), lambda b, h, i: (0, 0)),
            pl.BlockSpec((1, SB_DH), lambda b, h, i: (0, 0)),
        ],
        out_specs=pl.BlockSpec((SB_TILE, hw), lambda b, h, i: (b * nq + i, h)),
        out_shape=jax.ShapeDtypeStruct((t, BRANCH_WIDTH), BF16),
        scratch_shapes=[pltpu.VMEM((seq, hw), BF16), pltpu.VMEM((seq, hw), BF16)],
        compiler_params=_cparams(("parallel", "parallel", "arbitrary")),
        name="stickbreak",
    )(proj, proj, proj, q_gain.reshape(1, SB_DH), k_gain.reshape(1, SB_DH))


def _merge_kernel(ya_ref, yb_ref, yc_ref, wb_ref, ga_ref, gb_ref, gc_ref, o_ref):
    acc = _sigmoid(ga_ref[...]) * _dot(ya_ref[...], wb_ref[0])
    acc = acc + _sigmoid(gb_ref[...]) * _dot(yb_ref[...], wb_ref[1])
    acc = acc + _sigmoid(gc_ref[...]) * _dot(yc_ref[...], wb_ref[2])
    o_ref[...] = acc.astype(o_ref.dtype)


def _merge(ya, yb, yc, proj, w_branch, *, tm=1024, tn=512):
    t = ya.shape[0]
    tm = min(tm, t)
    nj = D_MODEL // tn
    g0 = COL_GATES * LANES // tn
    ysp = pl.BlockSpec((tm, BRANCH_WIDTH), lambda i, j: (i, 0))

    def gate(nb):
        return pl.BlockSpec((tm, tn), lambda i, j: (i, g0 + nb * nj + j))

    return pl.pallas_call(
        _merge_kernel,
        grid=(t // tm, nj),
        in_specs=[ysp, ysp, ysp,
                  pl.BlockSpec((N_BRANCH, BRANCH_WIDTH, tn), lambda i, j: (0, 0, j)),
                  gate(0), gate(1), gate(2)],
        out_specs=pl.BlockSpec((tm, tn), lambda i, j: (i, j)),
        out_shape=jax.ShapeDtypeStruct((t, D_MODEL), BF16),
        compiler_params=_cparams(("parallel", "arbitrary")),
        name="merge",
    )(ya, yb, yc, w_branch, proj, proj, proj)


def _out_proj_kernel(m_ref, w_ref, x_ref, g_ref, x1_ref, xn_ref):
    x1 = x_ref[...] + _dot(m_ref[...], w_ref[...])
    x1_ref[...] = x1
    ms = jnp.mean(x1 * x1, axis=-1, keepdims=True)
    xn_ref[...] = (x1 * lax.rsqrt(ms + EPS) * g_ref[...]).astype(BF16)


def _out_proj(merged, w_out, x2d, gain, *, tm=256):
    t, d = x2d.shape
    tm = min(tm, t)
    rows = pl.BlockSpec((tm, d), lambda i: (i, 0))
    return pl.pallas_call(
        _out_proj_kernel,
        grid=(t // tm,),
        in_specs=[rows, pl.BlockSpec((d, d), lambda i: (0, 0)), rows,
                  pl.BlockSpec((1, d), lambda i: (0, 0))],
        out_specs=[rows, rows],
        out_shape=[jax.ShapeDtypeStruct((t, d), F32), jax.ShapeDtypeStruct((t, d), BF16)],
        compiler_params=_cparams(("parallel",)),
        name="out_proj",
    )(merged, w_out, x2d, gain.reshape(1, d))


FFN_RB = 512
FFN_FC = 512
MOE_FC = 1024


FFN_ROW_STEP = 128


def _ffn_kernel(be_ref, used_ref, x_ref, w1_ref, w3_ref, w2_ref, *rest, residual, ragged, n_cast):
    rest = list(rest)
    res_ref = rest.pop(0) if residual else None
    cast_in = [rest.pop(0) for _ in range(n_cast)]
    o_ref = rest.pop(0)
    cast_out = [rest.pop(0) for _ in range(n_cast)]
    acc_ref, = rest
    i, c = pl.program_id(0), pl.program_id(1)
    rb = x_ref.shape[0]
    used = used_ref[i]

    for src, dst in zip(cast_in, cast_out):
        dst[...] = src[...].astype(dst.dtype)

    @pl.when(c == 0)
    def _():
        acc_ref[...] = jnp.zeros_like(acc_ref)

    def accumulate(nrows):
        x = x_ref[0:nrows, :]
        h1 = _dot(x, w1_ref[...])
        h3 = _dot(x, w3_ref[...])
        acc_ref[0:nrows, :] += _dot((_silu(h1) * h3).astype(BF16), w2_ref[...])

    if ragged:
        for nrows in range(FFN_ROW_STEP, rb + 1, FFN_ROW_STEP):
            pl.when(used == nrows)(functools.partial(accumulate, nrows))
    else:
        pl.when(used > 0)(functools.partial(accumulate, rb))

    @pl.when(c == pl.num_programs(1) - 1)
    def _():
        y = acc_ref[...]
        if residual:
            y = y + res_ref[...]
        o_ref[...] = y.astype(o_ref.dtype)


def _ffn(x_rows, w1, w3, w2, block_expert, block_used, residual=None, out_dtype=F32, fc=FFN_FC,
         ragged=False, cast=()):
    r, d = x_rows.shape
    f = w1.shape[-1]
    rb = min(FFN_RB, r)
    nblk, nc = r // rb, f // fc
    last = nc - 1

    def ccol(c, used):
        return jnp.where(used > 0, c, last)

    rows = pl.BlockSpec((rb, d), lambda i, c, be, us: (i, 0))
    in_specs = [
        rows,
        pl.BlockSpec((None, d, fc), lambda i, c, be, us: (be[i], 0, ccol(c, us[i]))),
        pl.BlockSpec((None, d, fc), lambda i, c, be, us: (be[i], 0, ccol(c, us[i]))),
        pl.BlockSpec((None, fc, d), lambda i, c, be, us: (be[i], ccol(c, us[i]), 0)),
    ]
    args = [x_rows, w1, w3, w2]
    if residual is not None:
        in_specs.append(rows)
        args.append(residual)
    cast_args, cast_specs, cast_shapes = _cast_specs(cast, nblk, nc)
    outs = pl.pallas_call(
        functools.partial(_ffn_kernel, residual=residual is not None, ragged=ragged, n_cast=len(cast)),
        grid_spec=pltpu.PrefetchScalarGridSpec(
            num_scalar_prefetch=2,
            grid=(nblk, nc),
            in_specs=in_specs + cast_specs,
            out_specs=[rows] + cast_specs,
            scratch_shapes=[pltpu.VMEM((rb, d), F32)],
        ),
        out_shape=[jax.ShapeDtypeStruct((r, d), out_dtype)] + cast_shapes,
        compiler_params=_cparams(("parallel", "arbitrary")),
        name="swiglu",
    )(block_expert, block_used, *args, *cast_args)
    return outs[0], [o.reshape(w.shape) for o, w in zip(outs[1:], cast)]


ROUTE_TT = 512
RT_W, RT_RANK, RT_MEMBER = 0, 8, 16


def _router_kernel(x_ref, g_ref, wr_ref, rowtab_ref, coltab_ref, cum_ref, count_ref):
    i = pl.program_id(0)
    tt = x_ref.shape[0]

    @pl.when(i == 0)
    def _():
        count_ref[...] = jnp.zeros_like(count_ref)

    x = x_ref[...]
    ms = jnp.mean(x * x, axis=-1, keepdims=True)
    xn = x * lax.rsqrt(ms + EPS) * g_ref[...]
    logits = jnp.dot(xn, wr_ref[...], preferred_element_type=F32, precision=lax.Precision.HIGHEST)
    lg = logits.T[0:N_EXPERTS, :]
    eidx = lax.broadcasted_iota(jnp.int32, lg.shape, 0).astype(F32)
    m1 = jnp.max(lg, axis=0, keepdims=True)
    i1 = jnp.min(jnp.where(lg == m1, eidx, float(N_EXPERTS)), axis=0, keepdims=True)
    sel1 = eidx == i1
    lg2 = jnp.where(sel1, -jnp.inf, lg)
    m2 = jnp.max(lg2, axis=0, keepdims=True)
    i2 = jnp.min(jnp.where(lg2 == m2, eidx, float(N_EXPERTS)), axis=0, keepdims=True)
    sel2 = eidx == i2
    e2 = jnp.exp(m2 - m1)
    den = 1.0 + e2
    wts = jnp.where(sel1, 1.0 / den, 0.0) + jnp.where(sel2, e2 / den, 0.0)
    member = (sel1 | sel2).astype(F32)

    srow = lax.broadcasted_iota(jnp.int32, (tt, tt), 0)
    scol = lax.broadcasted_iota(jnp.int32, (tt, tt), 1)
    before = (srow < scol).astype(BF16)
    base = count_ref[...]
    rank = _dot(member.astype(BF16), before) + base
    cum_ref[...] = jnp.broadcast_to(base, (N_EXPERTS, LANES))[None]
    count_ref[...] = base + jnp.sum(member, axis=1, keepdims=True)

    rowtab = jnp.concatenate([wts, rank, member], axis=0)
    rowtab_ref[...] = rowtab
    pad = jnp.zeros((LANES - 3 * N_EXPERTS, tt), F32)
    coltab_ref[...] = jnp.concatenate([rowtab, pad], axis=0).T


def _router(x1, gain, w_router):
    t, d = x1.shape
    tt = min(ROUTE_TT, t)
    nt = t // tt
    wr = jnp.zeros((d, LANES), F32).at[:, :N_EXPERTS].set(w_router.astype(F32))
    return pl.pallas_call(
        _router_kernel,
        grid=(nt,),
        in_specs=[pl.BlockSpec((tt, d), lambda i: (i, 0)),
                  pl.BlockSpec((1, d), lambda i: (0, 0)),
                  pl.BlockSpec((d, LANES), lambda i: (0, 0))],
        out_specs=[pl.BlockSpec((3 * N_EXPERTS, tt), lambda i: (0, i)),
                   pl.BlockSpec((tt, LANES), lambda i: (i, 0)),
                   pl.BlockSpec((1, N_EXPERTS, LANES), lambda i: (i, 0, 0))],
        out_shape=[jax.ShapeDtypeStruct((3 * N_EXPERTS, t), F32),
                   jax.ShapeDtypeStruct((t, LANES), F32),
                   jax.ShapeDtypeStruct((nt, N_EXPERTS, LANES), F32)],
        scratch_shapes=[pltpu.VMEM((N_EXPERTS, 1), F32)],
        compiler_params=_cparams(("arbitrary",)),
        name="router",
    )(x1, gain.reshape(1, d), wr)


PAIR_WIN = 256


def _gather_kernel(pb_ref, pt_ref, pe_ref, plo_ref, pfirst_ref, pskip_ref, pwin_ref, x_ref, rt_ref,
                   o_ref):
    g = pl.program_id(0)
    rb, tt = o_ref.shape[0], x_ref.shape[0]
    win = pwin_ref[g]

    @pl.when(pfirst_ref[g] == 1)
    def _():
        o_ref[...] = jnp.zeros_like(o_ref)

    def target_row():
        e = pe_ref[g]
        rt = rt_ref[...]
        ridx = lax.broadcasted_iota(jnp.int32, rt.shape, 0)
        rank = jnp.sum(jnp.where(ridx == RT_RANK + e, rt, 0.0), axis=0, keepdims=True)
        member = jnp.sum(jnp.where(ridx == RT_MEMBER + e, rt, 0.0), axis=0, keepdims=True)
        return jnp.where(member > 0.5, rank - plo_ref[g].astype(F32), -1.0)

    @pl.when((pskip_ref[g] == 0) & (win >= 0))
    def _():
        w0 = pl.multiple_of(win, LANES)
        rows = (lax.broadcasted_iota(jnp.int32, (PAIR_WIN, tt), 0) + w0).astype(F32)
        onehot = (rows == target_row()).astype(BF16)
        o_ref[pl.ds(w0, PAIR_WIN), :] += _dot(onehot, x_ref[...]).astype(o_ref.dtype)

    @pl.when((pskip_ref[g] == 0) & (win < 0))
    def _():
        rows = lax.broadcasted_iota(jnp.int32, (rb, tt), 0).astype(F32)
        onehot = (rows == target_row()).astype(BF16)
        o_ref[...] += _dot(onehot, x_ref[...]).astype(o_ref.dtype)


def _gather_rows(xn, rowtab, pairs, nb):
    t, d = xn.shape
    tt = min(ROUTE_TT, t)
    rb = FFN_RB
    npairs = pairs[0].shape[0]
    return pl.pallas_call(
        _gather_kernel,
        grid_spec=pltpu.PrefetchScalarGridSpec(
            num_scalar_prefetch=7,
            grid=(npairs,),
            in_specs=[pl.BlockSpec((tt, d), lambda g, pb, pt, *_: (pt[g], 0)),
                      pl.BlockSpec((3 * N_EXPERTS, tt), lambda g, pb, pt, *_: (0, pt[g]))],
            out_specs=pl.BlockSpec((rb, d), lambda g, pb, *_: (pb[g], 0)),
        ),
        out_shape=jax.ShapeDtypeStruct((nb * rb, d), BF16),
        compiler_params=_cparams(("arbitrary",)),
        name="moe_gather",
    )(*pairs, xn, rowtab)


def _combine_kernel(pb_ref, pt_ref, pe_ref, plo_ref, pfirst_ref, pskip_ref, pwin_ref, y_ref, ct_ref,
                    x_ref, o_ref):
    g = pl.program_id(0)
    tt, rb = o_ref.shape[0], y_ref.shape[0]
    win = pwin_ref[g]

    @pl.when(pfirst_ref[g] == 1)
    def _():
        o_ref[...] = x_ref[...]

    def weight_and_target():
        e = pe_ref[g]
        ct = ct_ref[...]
        lane = lax.broadcasted_iota(jnp.int32, ct.shape, 1)
        wcol = jnp.sum(jnp.where(lane == RT_W + e, ct, 0.0), axis=1, keepdims=True)
        rank = jnp.sum(jnp.where(lane == RT_RANK + e, ct, 0.0), axis=1, keepdims=True)
        member = jnp.sum(jnp.where(lane == RT_MEMBER + e, ct, 0.0), axis=1, keepdims=True)
        return wcol, jnp.where(member > 0.5, rank - plo_ref[g].astype(F32), -1.0)

    @pl.when((pskip_ref[g] == 0) & (win >= 0))
    def _():
        w0 = pl.multiple_of(win, LANES)
        wcol, target = weight_and_target()
        cols = (lax.broadcasted_iota(jnp.int32, (tt, PAIR_WIN), 1) + w0).astype(F32)
        onehot = (cols == target).astype(BF16)
        o_ref[...] += wcol * _dot(onehot, y_ref[pl.ds(w0, PAIR_WIN), :])

    @pl.when((pskip_ref[g] == 0) & (win < 0))
    def _():
        wcol, target = weight_and_target()
        cols = lax.broadcasted_iota(jnp.int32, (tt, rb), 1).astype(F32)
        onehot = (cols == target).astype(BF16)
        o_ref[...] += wcol * _dot(onehot, y_ref[...])


def _combine(y_rows, coltab, x1, pairs):
    t, d = x1.shape
    tt = min(ROUTE_TT, t)
    rb = FFN_RB
    npairs = pairs[0].shape[0]
    return pl.pallas_call(
        _combine_kernel,
        grid_spec=pltpu.PrefetchScalarGridSpec(
            num_scalar_prefetch=7,
            grid=(npairs,),
            in_specs=[pl.BlockSpec((rb, d), lambda g, pb, *_: (pb[g], 0)),
                      pl.BlockSpec((tt, LANES), lambda g, pb, pt, *_: (pt[g], 0)),
                      pl.BlockSpec((tt, d), lambda g, pb, pt, *_: (pt[g], 0))],
            out_specs=pl.BlockSpec((tt, d), lambda g, pb, pt, *_: (pt[g], 0)),
        ),
        out_shape=jax.ShapeDtypeStruct((t, d), F32),
        compiler_params=_cparams(("arbitrary",)),
        name="moe_combine",
    )(*pairs, y_rows, coltab, x1)


def _pair_tables(cum, counts, nb, order):
    nt = cum.shape[0]
    rb = FFN_RB
    npairs = nb + N_EXPERTS * nt
    padded = (counts + rb - 1) // rb * rb
    seg_end = jnp.cumsum(padded)
    seg_start = seg_end - padded
    blk = jnp.arange(nb, dtype=jnp.int32)
    blk_row = blk * rb
    blk_e = jnp.minimum(jnp.searchsorted(seg_end, blk_row, side='right'), N_EXPERTS - 1).astype(jnp.int32)
    blk_lo = blk_row - seg_start[blk_e]
    blk_valid = (blk_row < seg_end[-1]) & (blk_lo < counts[blk_e])
    blk_hi = jnp.minimum(blk_lo + rb, counts[blk_e])
    tile_lo = cum[:, blk_e]
    tile_hi = jnp.concatenate([cum[1:], counts[None, :]], axis=0)[:, blk_e]
    overlap = blk_valid[None, :] & (tile_lo < blk_hi[None, :]) & (tile_hi > blk_lo[None, :]) \
        & (tile_hi > tile_lo)
    if order == 'block':
        flat = overlap.T.reshape(-1)
        idx = jnp.nonzero(flat, size=npairs, fill_value=-1)[0].astype(jnp.int32)
        n_valid = jnp.sum(flat.astype(jnp.int32))
        last = idx[jnp.maximum(n_valid - 1, 0)]
        idx = jnp.where(idx < 0, last, idx)
        pb, pt = idx // nt, idx % nt
        major = pb
    else:
        flat = overlap.reshape(-1)
        idx = jnp.nonzero(flat, size=npairs, fill_value=-1)[0].astype(jnp.int32)
        n_valid = jnp.sum(flat.astype(jnp.int32))
        last = idx[jnp.maximum(n_valid - 1, 0)]
        idx = jnp.where(idx < 0, last, idx)
        pt, pb = idx // nb, idx % nb
        major = pt
    pos = jnp.arange(npairs, dtype=jnp.int32)
    skip = (pos >= n_valid).astype(jnp.int32)
    first = jnp.concatenate([jnp.ones((1,), jnp.int32),
                             (major[1:] != major[:-1]).astype(jnp.int32)]) * (1 - skip)
    r0 = jnp.maximum(tile_lo[pt, pb] - blk_lo[pb], 0)
    r1 = jnp.minimum(tile_hi[pt, pb], blk_hi[pb]) - blk_lo[pb]
    w0 = jnp.minimum(r0 // LANES * LANES, rb - PAIR_WIN)
    win = jnp.where(r1 <= w0 + PAIR_WIN, w0, -1)
    pairs = (pb.astype(jnp.int32), pt.astype(jnp.int32), blk_e[pb], blk_lo[pb].astype(jnp.int32),
             first.astype(jnp.int32), skip, win.astype(jnp.int32))
    step = FFN_ROW_STEP
    blk_used = jnp.where(blk_valid, (blk_hi - blk_lo + step - 1) // step * step, 0)
    return pairs, blk_e, blk_used.astype(jnp.int32)


def _moe(x1, xn, gain, w_router, w1, w3, w2):
    t, d = x1.shape
    rowtab, coltab, cum3 = _router(x1, gain, w_router)
    cum = cum3[:, :, 0].astype(jnp.int32)
    tt = min(ROUTE_TT, t)
    last_members = jnp.sum(rowtab[RT_MEMBER:RT_MEMBER + N_EXPERTS, t - tt:], axis=1).astype(jnp.int32)
    counts = cum[-1] + last_members
    nb = (t * TOP_K) // FFN_RB + N_EXPERTS
    pairs_g, blk_e, blk_used = _pair_tables(cum, counts, nb, 'block')
    pairs_c, _, _ = _pair_tables(cum, counts, nb, 'tile')
    x_rows = _gather_rows(xn, rowtab, pairs_g, nb)
    y_rows, _ = _ffn(x_rows, w1, w3, w2, blk_e, blk_used, out_dtype=BF16, fc=MOE_FC, ragged=True)
    return _combine(y_rows, coltab, x1, pairs_c)


def _split_w_in(w):
    main = jnp.concatenate([w[:, 0:4096], w[:, 4112:7184], w[:, 7200:16416]], axis=1).astype(BF16)
    small = jnp.concatenate([w[:, 4096:4112], w[:, 7184:7200],
                             jnp.zeros((w.shape[0], LANES - 32), w.dtype)], axis=1).astype(BF16)
    return main, small


def kernel(x, norm_mix, w_in, conv_w, gdn_a_log, gdn_dt_bias, gdn_norm, gla_w_gate, gla_b_gate,
           gla_norm, sb_q_norm, sb_k_norm, w_branch, w_out, norm_ffn, ffn_w1, ffn_w3, ffn_w2,
           moe_router, moe_w1, moe_w3, moe_w2):
    batch, seq, d = x.shape
    t = batch * seq
    depth = w_in.shape[0]
    x2 = x.reshape(t, d)
    nblk = t // min(FFN_RB, t)
    expert_bf16 = {}

    def ride(layer, name, w, n0, n1):
        ok = layer % 2 == 1 and _cast_blocks(w, n0, n1) is not None
        return ((layer, name), w) if ok else None

    for layer in range(depth):
        w_main, w_small = _split_w_in(w_in[layer])
        i = layer // 2
        if layer % 2 == 0 and layer + 1 < depth:
            job = ride(layer + 1, 'w1', moe_w1[i], *_in_proj_grid(t))
        elif layer % 2 == 1:
            job = None if (layer, 'w2') in expert_bf16 else ride(layer, 'w2', moe_w2[i], *_in_proj_grid(t))
        else:
            job = None
        proj, small, casted = _in_proj(x2, norm_mix[layer], w_main, w_small,
                                       cast=(job[1],) if job else ())
        if job:
            expert_bf16[job[0]] = casted[0]
        ya = _gdn(proj, small, conv_w[layer], gdn_a_log[layer], gdn_dt_bias[layer], gdn_norm[layer],
                  batch, seq)
        yb = _gla(proj, small, gla_w_gate[layer], gla_b_gate[layer], gla_norm[layer], batch, seq)
        yc = _sb(proj, sb_q_norm[layer], sb_k_norm[layer], batch, seq)
        merged = _merge(ya, yb, yc, proj, w_branch[layer].astype(BF16))
        x1, xn = _out_proj(merged, w_out[layer].astype(BF16), x2, norm_ffn[layer])
        if layer % 2 == 0:
            job = ride(layer + 1, 'w3', moe_w3[i], nblk, D_FF // FFN_FC) if layer + 1 < depth else None
            x2, casted = _ffn(xn, ffn_w1[i][None].astype(BF16), ffn_w3[i][None].astype(BF16),
                              ffn_w2[i][None].astype(BF16), jnp.zeros((nblk,), jnp.int32),
                              jnp.full((nblk,), min(FFN_RB, t), jnp.int32), residual=x1,
                              cast=(job[1],) if job else ())
            if job:
                expert_bf16[job[0]] = casted[0]
        else:
            w1, w3, w2 = (expert_bf16.pop((layer, name), None) for name in ('w1', 'w3', 'w2'))
            w1 = moe_w1[i].astype(BF16) if w1 is None else w1
            w3 = moe_w3[i].astype(BF16) if w3 is None else w3
            w2 = moe_w2[i].astype(BF16) if w2 is None else w2
            x2 = _moe(x1, xn, norm_ffn[layer], moe_router[i], w1, w3, w2)
    return x2.reshape(batch, seq, d)
```

```python
import functools

import jax
import jax.numpy as jnp
from jax import lax
from jax.experimental import pallas as pl
from jax.experimental.pallas import tpu as pltpu

F32 = jnp.float32
BF16 = jnp.bfloat16

D_MODEL = 2048
GDN_HEADS, GDN_DK, GDN_DV, CONV_K, CHUNK = 8, 128, 128, 4, 64
GLA_HEADS, GLA_DK, GLA_DV, GLA_RANK, GLA_NORMALIZER = 4, 128, 256, 16, 16.0
SB_HEADS, SB_DH, SB_BLOCK = 8, 128, 128
N_BRANCH, BRANCH_WIDTH = 3, 1024
D_FF, N_EXPERTS, TOP_K = 7168, 8, 2
EPS = 1e-6

LANES = 128
SUBLANES = 8
VMEM_LIMIT = 48 * 1024 * 1024

COL_GDN_Q, COL_GDN_K, COL_GDN_V, COL_GDN_Z = 0, 8, 16, 24
COL_GLA_Q, COL_GLA_K, COL_GLA_V, COL_GLA_R = 32, 36, 40, 48
COL_SB_Q, COL_SB_K, COL_SB_V = 56, 64, 72
COL_GATES = 80
N_MAIN = 128 * LANES
SM_A, SM_B, SM_G = 0, 8, 16


def _cparams(sem):
    return pltpu.CompilerParams(dimension_semantics=sem, vmem_limit_bytes=VMEM_LIMIT)


def _dot(a, b):
    return jnp.dot(a, b, preferred_element_type=F32)


def _dot_nt(a, b):
    return lax.dot_general(a, b, (((1,), (1,)), ((), ())), preferred_element_type=F32)


def _dot_tn(a, b):
    return lax.dot_general(a, b, (((0,), (0,)), ((), ())), preferred_element_type=F32)


def _sigmoid(x):
    return 1.0 / (1.0 + jnp.exp(-x))


def _silu(x):
    return x * _sigmoid(x)


def _softplus(x):
    return jnp.maximum(x, 0.0) + jnp.log1p(jnp.exp(-jnp.abs(x)))


def _log_sigmoid(x):
    return -_softplus(-x)


def _seg_cumsum_rows(x, seg):
    pos = lax.broadcasted_iota(jnp.int32, x.shape, 0) & (seg - 1)
    s = 1
    while s < seg:
        x = x + jnp.where(pos >= s, pltpu.roll(x, s, axis=0), 0.0)
        s *= 2
    return x


def _cast_blocks(w, n0, n1):
    w2d = w.reshape(-1, w.shape[-1])
    rows, cols = w2d.shape
    pack = 2 * SUBLANES
    if rows % (n0 * n1) == 0 and (rows // (n0 * n1)) % pack == 0:
        return w2d, (rows // (n0 * n1), cols), lambda i, j, *_: (i * n1 + j, 0)
    if rows % n0 == 0 and cols % n1 == 0 and (rows // n0) % pack == 0 and (cols // n1) % LANES == 0:
        return w2d, (rows // n0, cols // n1), lambda i, j, *_: (i, j)
    return None


def _cast_specs(cast, n0, n1):
    plans = [_cast_blocks(w, n0, n1) for w in cast]
    specs = [pl.BlockSpec(blk, imap) for _, blk, imap in plans]
    shapes = [jax.ShapeDtypeStruct(w2d.shape, BF16) for w2d, _, _ in plans]
    return [w2d for w2d, _, _ in plans], specs, shapes


def _in_proj_kernel(x_ref, g_ref, wm_ref, ws_ref, *rest, n_cast):
    cast_in, rest = rest[:n_cast], rest[n_cast:]
    proj_ref, small_ref = rest[:2]
    cast_out, (xn_ref,) = rest[2:2 + n_cast], rest[2 + n_cast:]
    for src, dst in zip(cast_in, cast_out):
        dst[...] = src[...].astype(dst.dtype)

    @pl.when(pl.program_id(1) == 0)
    def _():
        x = x_ref[...]
        ms = jnp.mean(x * x, axis=-1, keepdims=True)
        xn = (x * lax.rsqrt(ms + EPS) * g_ref[...]).astype(BF16)
        xn_ref[...] = xn
        small_ref[...] = _dot_nt(xn, ws_ref[...])

    proj_ref[...] = _dot_nt(xn_ref[...], wm_ref[...])


IN_PROJ_TM, IN_PROJ_TN = 1024, 1024


def _in_proj_grid(t):
    return t // min(IN_PROJ_TM, t), N_MAIN // IN_PROJ_TN


def _in_proj(x2d, gain, w_main, w_small, cast=()):
    t, d = x2d.shape
    tm, tn = min(IN_PROJ_TM, t), IN_PROJ_TN
    n = w_main.shape[0]
    grid = _in_proj_grid(t)
    cast_args, cast_specs, cast_shapes = _cast_specs(cast, *grid)
    outs = pl.pallas_call(
        functools.partial(_in_proj_kernel, n_cast=len(cast)),
        grid=grid,
        in_specs=[
            pl.BlockSpec((tm, d), lambda i, j: (i, 0)),
            pl.BlockSpec((1, d), lambda i, j: (0, 0)),
            pl.BlockSpec((tn, d), lambda i, j: (j, 0)),
            pl.BlockSpec((LANES, d), lambda i, j: (0, 0)),
        ] + cast_specs,
        out_specs=[
            pl.BlockSpec((tm, tn), lambda i, j: (i, j)),
            pl.BlockSpec((tm, LANES), lambda i, j: (i, 0)),
        ] + cast_specs,
        out_shape=[jax.ShapeDtypeStruct((t, n), F32), jax.ShapeDtypeStruct((t, LANES), F32)] + cast_shapes,
        scratch_shapes=[pltpu.VMEM((tm, d), BF16)],
        compiler_params=_cparams(("parallel", "arbitrary")),
        name="in_proj",
    )(x2d, gain.reshape(1, d), w_main, w_small, *cast_args)
    return outs[0], outs[1], [o.reshape(w.shape) for o, w in zip(outs[2:], cast)]


GDN_SBLK = 256


def _causal_conv_silu(x, prev, w):
    n = x.shape[0]
    row8 = lax.broadcasted_iota(jnp.int32, (SUBLANES, LANES), 0)
    acc = x * w[CONV_K - 1:CONV_K, :]
    for j in range(1, CONV_K):
        xs = pltpu.roll(x, j, axis=0)
        ps = pltpu.roll(prev, j, axis=0)
        top = jnp.where(row8 < j, ps, xs[0:SUBLANES, :])
        xs = jnp.concatenate([top, xs[SUBLANES:n, :]], axis=0)
        acc = acc + xs * w[CONV_K - 1 - j:CONV_K - j, :]
    return _silu(acc)


def _unit_lower_inverses(l_mats, row, col):
    eye = (row == col).astype(F32)
    blk16 = (row >> 4) == (col >> 4)
    blk32 = (row >> 5) == (col >> 5)
    bf = lambda ms: [m.astype(BF16) for m in ms]
    mm = lambda xs, ys: [_dot(a, b) for a, b in zip(xs, ys)]

    d32 = [jnp.where(blk16, l, 0.0) for l in l_mats]
    d = bf(d32)
    e1 = bf([jnp.where(blk32 & (~blk16), l, 0.0) for l in l_mats])
    e2 = bf([jnp.where(blk32, 0.0, l) for l in l_mats])
    d2 = bf(mm(d, d))
    d4 = bf(mm(d2, d2))
    d8 = bf(mm(d4, d4))
    x = [eye - m for m in d32]
    for power in (d2, d4, d8):
        x = [a + b for a, b in zip(x, mm(bf(x), power))]
    for e in (e1, e2):
        xb = bf(x)
        x = [a - b for a, b in zip(x, mm(bf(mm(xb, e)), xb))]
    return x


GDN_HG = GDN_HEADS


def _gdn_kernel(q_ref, k_ref, v_ref, z_ref, sm_ref, wq_ref, wk_ref, wv_ref, alog_ref, dtb_ref,
                gn_ref, o_ref, state_ref, tail_ref, rhs_ref, u_ref, w_ref, qk_ref, qd_ref, kd_ref,
                gl_ref, vn_ref):
    n = GDN_SBLK
    pair = 2 * CHUNK

    @pl.when(pl.program_id(1) == 0)
    def _():
        state_ref[...] = jnp.zeros_like(state_ref)
        tail_ref[...] = jnp.zeros_like(tail_ref)

    sm = sm_ref[...]
    g_all = _seg_cumsum_rows(-jnp.exp(alog_ref[...]) * _softplus(sm + dtb_ref[...]), CHUNK)
    eg_all = jnp.exp(g_all)
    beta_all = _sigmoid(sm)
    row = lax.broadcasted_iota(jnp.int32, (pair, pair), 0)
    col = lax.broadcasted_iota(jnp.int32, (pair, pair), 1)
    same = (row >> 6) == (col >> 6)
    causal = same & (row >= col)
    strict = same & (row > col)
    first = lax.broadcasted_iota(jnp.int32, (pair, LANES), 0) < CHUNK

    l_mats = []
    for j in range(GDN_HG):
        hl = slice(j * LANES, (j + 1) * LANES)

        def conv(x_ref, w_ref_, idx):
            x = x_ref[:, hl]
            y = _causal_conv_silu(x, tail_ref[idx, :, hl], w_ref_[j])
            tail_ref[idx, :, hl] = x[n - SUBLANES:n, :]
            return y

        q = conv(q_ref, wq_ref, 0)
        k = conv(k_ref, wk_ref, 1)
        v = conv(v_ref, wv_ref, 2)
        q = q * lax.rsqrt(jnp.sum(q * q, axis=-1, keepdims=True) + EPS) * (GDN_DK ** -0.5)
        k = k * lax.rsqrt(jnp.sum(k * k, axis=-1, keepdims=True) + EPS)
        beta = beta_all[:, SM_B + j:SM_B + j + 1]
        eg = eg_all[:, SM_A + j:SM_A + j + 1]
        g = jnp.broadcast_to(g_all[:, SM_A + j:SM_A + j + 1], (n, LANES))
        kb = k * beta
        vb = v * beta

        rhs_ref[j, :, 0:GDN_DV] = vb.astype(BF16)
        rhs_ref[j, :, GDN_DV:] = (kb * eg).astype(BF16)
        qd_ref[j] = (q * eg).astype(BF16)
        for p in range(n // pair):
            sl = slice(p * pair, (p + 1) * pair)
            g_p = g[sl, :]
            g_rowmat = g_p.T
            decay = jnp.exp(jnp.where(causal, g_p - g_rowmat, -jnp.inf))
            k_p = k[sl, :].astype(BF16)
            kk = _dot_nt(kb[sl, :].astype(BF16), k_p)
            l_mats.append(jnp.where(strict, kk * decay, 0.0))
            qk = _dot_nt(q[sl, :].astype(BF16), k_p) * decay
            qk_ref[j, sl, :] = jnp.where(causal, qk, 0.0).astype(BF16)
            g_last = jnp.where(first, g_p[CHUNK - 1:CHUNK, :], g_p[pair - 1:pair, :])
            kd_ref[j, sl, :] = (k[sl, :] * jnp.exp(g_last - g_p)).astype(BF16)
            gl_ref[j, sl, :] = jnp.exp(g_last)

    x_invs = _unit_lower_inverses(l_mats, row, col)
    for idx, x_inv in enumerate(x_invs):
        j, p = divmod(idx, n // pair)
        sl = slice(p * pair, (p + 1) * pair)
        sol = _dot(x_inv.astype(BF16), rhs_ref[j, sl, :])
        u_ref[j, sl, :] = sol[:, 0:GDN_DV]
        w_ref[j, sl, :] = sol[:, GDN_DV:].astype(BF16)

    heads = range(GDN_HG)
    states = [state_ref[j] for j in heads]
    for c in range(n // CHUNK):
        sl = slice(c * CHUNK, (c + 1) * CHUNK)
        half = c % 2
        if half == 0:
            vn_ref[...] = jnp.zeros_like(vn_ref)
        sbs = [s.astype(BF16) for s in states]
        v_news = [u_ref[j, sl, :] - _dot(w_ref[j, sl, :], sbs[j]) for j in heads]
        vnbs = [v.astype(BF16) for v in v_news]
        for j in heads:
            vn_ref[j, half * CHUNK:(half + 1) * CHUNK, :] = vnbs[j]
        outs = [_dot(qd_ref[j, sl, :], sbs[j]) + _dot(qk_ref[j, sl, :], vn_ref[j]) for j in heads]
        states = [states[j] * gl_ref[j, c * CHUNK:c * CHUNK + 1, :] + _dot_tn(kd_ref[j, sl, :], vnbs[j])
                  for j in heads]
        for j in heads:
            hl = slice(j * LANES, (j + 1) * LANES)
            o = outs[j]
            ms = jnp.mean(o * o, axis=-1, keepdims=True)
            o = o * lax.rsqrt(ms + EPS) * gn_ref[...]
            o_ref[sl, hl] = (o * _silu(z_ref[sl, hl])).astype(o_ref.dtype)
    for j in heads:
        state_ref[j] = states[j]


def _gdn(proj, small, conv_w, a_log, dt_bias, gnorm, batch, seq):
    t = proj.shape[0]
    n = GDN_SBLK
    hgn = GDN_HG
    ns = seq // n
    cw = conv_w.reshape(CONV_K, 3 * GDN_HEADS, LANES).transpose(1, 0, 2)
    lane_pad = (SM_A, LANES - SM_A - GDN_HEADS)
    alog = jnp.pad(a_log.astype(F32), lane_pad).reshape(1, LANES)
    dtb = jnp.pad(dt_bias.astype(F32), lane_pad).reshape(1, LANES)

    def col(off):
        return pl.BlockSpec((n, hgn * LANES), lambda b, s: (b * ns + s, off // hgn))

    def cws(off):
        return pl.BlockSpec((hgn, CONV_K, LANES), lambda b, s: (off // hgn, 0, 0))

    par = pl.BlockSpec((1, LANES), lambda b, s: (0, 0))
    return pl.pallas_call(
        _gdn_kernel,
        grid=(batch, ns),
        in_specs=[col(COL_GDN_Q), col(COL_GDN_K), col(COL_GDN_V), col(COL_GDN_Z),
                  pl.BlockSpec((n, LANES), lambda b, s: (b * ns + s, 0)),
                  cws(0), cws(GDN_HEADS), cws(2 * GDN_HEADS), par, par, par],
        out_specs=pl.BlockSpec((n, hgn * LANES), lambda b, s: (b * ns + s, 0)),
        out_shape=jax.ShapeDtypeStruct((t, BRANCH_WIDTH), BF16),
        scratch_shapes=[
            pltpu.VMEM((hgn, GDN_DK, GDN_DV), F32),
            pltpu.VMEM((3, SUBLANES, hgn * LANES), F32),
            pltpu.VMEM((hgn, n, GDN_DV + GDN_DK), BF16),
            pltpu.VMEM((hgn, n, LANES), F32),
            pltpu.VMEM((hgn, n, LANES), BF16),
            pltpu.VMEM((hgn, n, LANES), BF16),
            pltpu.VMEM((hgn, n, LANES), BF16),
            pltpu.VMEM((hgn, n, LANES), BF16),
            pltpu.VMEM((hgn, n, LANES), F32),
            pltpu.VMEM((hgn, 2 * CHUNK, LANES), BF16),
        ],
        compiler_params=_cparams(("parallel", "arbitrary")),
        name="gdn",
    )(proj, proj, proj, proj, small, cw, cw, cw, alog, dtb, gnorm.reshape(1, LANES))


GLA_SBLK = 256


def _gla_kernel(q_ref, k_ref, v_ref, r_ref, sm_ref, wg_ref, bg_ref, gn_ref, o_ref, state_ref):
    n = GLA_SBLK
    heads = range(GLA_HEADS)
    chunks = range(n // CHUNK)

    @pl.when(pl.program_id(1) == 0)
    def _():
        state_ref[...] = jnp.zeros_like(state_ref)

    pre = _dot(sm_ref[...].astype(BF16), wg_ref[...]) + bg_ref[...]
    log_g = _log_sigmoid(pre) / GLA_NORMALIZER
    gcum = _seg_cumsum_rows(log_g, CHUNK)
    row = lax.broadcasted_iota(jnp.int32, (CHUNK, CHUNK), 0)
    col = lax.broadcasted_iota(jnp.int32, (CHUNK, CHUNK), 1)
    causal = row >= col

    intra, update, q_dec, g_last = {}, {}, {}, {}
    for h in heads:
        kl = slice(h * GLA_DK, (h + 1) * GLA_DK)
        vl = slice(h * GLA_DV, (h + 1) * GLA_DV)
        g_h = gcum[:, kl]
        g_t = g_h.T
        q = q_ref[:, kl] * (GLA_DK ** -0.5)
        k = k_ref[:, kl]
        for c in chunks:
            sl = slice(c * CHUNK, (c + 1) * CHUNK)
            g_c = g_h[sl, :]
            ref = g_c[CHUNK // 2 - 1:CHUNK // 2, :]
            g_end = g_c[CHUNK - 1:CHUNK, :]
            q_c, k_c = q[sl, :], k[sl, :]
            v_c = v_ref[sl, vl].astype(BF16)
            a = _dot_nt((q_c * jnp.exp(g_c - ref)).astype(BF16), (k_c * jnp.exp(ref - g_c)).astype(BF16))
            intra[h, c] = _dot(jnp.where(causal, a, 0.0).astype(BF16), v_c)
            update[h, c] = _dot_tn((k_c * jnp.exp(g_end - g_c)).astype(BF16), v_c)
            q_dec[h, c] = (q_c * jnp.exp(g_c)).astype(BF16)
            g_last[h, c] = jnp.exp(g_t[:, (c + 1) * CHUNK - 1:(c + 1) * CHUNK])

    for h in heads:
        vl = slice(h * GLA_DV, (h + 1) * GLA_DV)
        state = state_ref[h]
        for c in chunks:
            sl = slice(c * CHUNK, (c + 1) * CHUNK)
            o = _dot(q_dec[h, c], state.astype(BF16)) + intra[h, c]
            state = state * g_last[h, c] + update[h, c]
            ms = jnp.mean(o * o, axis=-1, keepdims=True)
            o = o * lax.rsqrt(ms + EPS) * gn_ref[...]
            o_ref[sl, vl] = (o * _silu(r_ref[sl, vl])).astype(o_ref.dtype)
        state_ref[h] = state


def _gla(proj, small, w_gate, b_gate, gnorm, batch, seq):
    t = proj.shape[0]
    n = GLA_SBLK
    ns = seq // n
    kw, vw = GLA_HEADS * GLA_DK, GLA_HEADS * GLA_DV
    wg = jnp.zeros((LANES, kw), BF16).at[SM_G:SM_G + GLA_RANK, :].set(w_gate.astype(BF16))
    rows = lambda b, s: b * ns + s
    return pl.pallas_call(
        _gla_kernel,
        grid=(batch, ns),
        in_specs=[
            pl.BlockSpec((n, kw), lambda b, s: (rows(b, s), COL_GLA_Q * LANES // kw)),
            pl.BlockSpec((n, kw), lambda b, s: (rows(b, s), COL_GLA_K * LANES // kw)),
            pl.BlockSpec((n, vw), lambda b, s: (rows(b, s), COL_GLA_V * LANES // vw)),
            pl.BlockSpec((n, vw), lambda b, s: (rows(b, s), COL_GLA_R * LANES // vw)),
            pl.BlockSpec((n, LANES), lambda b, s: (rows(b, s), 0)),
            pl.BlockSpec((LANES, kw), lambda b, s: (0, 0)),
            pl.BlockSpec((1, kw), lambda b, s: (0, 0)),
            pl.BlockSpec((1, GLA_DV), lambda b, s: (0, 0)),
        ],
        out_specs=pl.BlockSpec((n, vw), lambda b, s: (rows(b, s), 0)),
        out_shape=jax.ShapeDtypeStruct((t, BRANCH_WIDTH), BF16),
        scratch_shapes=[pltpu.VMEM((GLA_HEADS, GLA_DK, GLA_DV), F32)],
        compiler_params=_cparams(("parallel", "arbitrary")),
        name="gla",
    )(proj, proj, proj, proj, small, wg, b_gate.reshape(1, -1).astype(F32), gnorm.reshape(1, GLA_DV))


def _split_hi_lo(x):
    hi = x.astype(BF16)
    lo = (x - hi.astype(F32)).astype(BF16)
    return hi, lo


SB_TILE = 256
SB_EXIT = -104.0
SB_HG = 4


def _sb_kernel(q_ref, k_ref, v_ref, qg_ref, kg_ref, o_ref, kn_ref, vb_ref):
    qi = pl.program_id(2)
    tl = SB_TILE

    heads = range(SB_HG)
    hl = [slice(h * SB_DH, (h + 1) * SB_DH) for h in heads]

    @pl.when(qi == 0)
    def _():
        for h in heads:
            kf = k_ref[:, hl[h]]
            ms = jnp.mean(kf * kf, axis=-1, keepdims=True)
            kn_ref[:, hl[h]] = (kf * lax.rsqrt(ms + EPS) * kg_ref[...]).astype(BF16)
        vb_ref[...] = v_ref[...].astype(BF16)

    qns = []
    for h in heads:
        q = q_ref[:, hl[h]]
        ms = jnp.mean(q * q, axis=-1, keepdims=True)
        qns.append((q * lax.rsqrt(ms + EPS) * qg_ref[...]).astype(BF16))
    row = lax.broadcasted_iota(jnp.int32, (tl, tl), 0)
    col = lax.broadcasted_iota(jnp.int32, (tl, tl), 1)
    strict = col < row
    after = (row > col).astype(BF16)
    scale = SB_DH ** -0.5

    def tile(j, carries, accs, diag):
        off = pl.multiple_of(j * tl, tl)
        zs = [_dot_nt(qns[h], kn_ref[pl.ds(off, tl), hl[h]]) * scale for h in heads]
        log_betas = [jnp.minimum(z, 0.0) - jnp.log(1.0 + jnp.exp(-jnp.abs(z))) for z in zs]
        log_1ms = [lb - z for lb, z in zip(log_betas, zs)]
        if diag:
            log_1ms = [jnp.where(strict, l, 0.0) for l in log_1ms]
        parts = [_split_hi_lo(l) for l in log_1ms]
        tails = [_dot(hi, after) + _dot(lo, after) + c for (hi, lo), c in zip(parts, carries)]
        attns = [jnp.exp(lb + t) for lb, t in zip(log_betas, tails)]
        if diag:
            attns = [jnp.where(strict, a, 0.0) for a in attns]
        accs = [accs[h] + _dot(attns[h].astype(BF16), vb_ref[pl.ds(off, tl), hl[h]]) for h in heads]
        carries = [c + jnp.sum(l, axis=-1, keepdims=True) for c, l in zip(carries, log_1ms)]
        return carries, accs

    def unfinished(carries):
        worst = carries[0]
        for c in carries[1:]:
            worst = jnp.maximum(worst, c)
        return jnp.max(worst) > SB_EXIT

    carries, accs = tile(qi, [jnp.zeros((tl, 1), F32)] * SB_HG, [jnp.zeros((tl, SB_DH), F32)] * SB_HG, True)

    def more(state):
        return jnp.logical_and(state[0] >= 0, state[1])

    def body(state):
        j, _, carries, accs = state
        carries, accs = tile(j, list(carries), list(accs), False)
        return j - 1, unfinished(carries), tuple(carries), tuple(accs)

    state = lax.while_loop(more, body, (qi - 1, unfinished(carries), tuple(carries), tuple(accs)))
    for h in heads:
        o_ref[:, hl[h]] = state[3][h].astype(o_ref.dtype)


def _sb(proj, q_gain, k_gain, batch, seq):
    t = proj.shape[0]
    nq = seq // SB_TILE
    hw = SB_HG * SB_DH
    return pl.pallas_call(
        _sb_kernel,
        grid=(batch, SB_HEADS // SB_HG, nq),
        in_specs=[
            pl.BlockSpec((SB_TILE, hw), lambda b, h, i: (b * nq + i, COL_SB_Q // SB_HG + h)),
            pl.BlockSpec((seq, hw), lambda b, h, i: (b, COL_SB_K // SB_HG + h)),
            pl.BlockSpec((seq, hw), lambda b, h, i: (b, COL_SB_V // SB_HG + h)),
            pl.BlockSpec((1, SB_DH), lambda b, h, i: (0, 0)),
            pl.BlockSpec((1, SB_DH), lambda b, h, i: (0, 0)),
        ],
        out_specs=pl.BlockSpec((SB_TILE, hw), lambda b, h, i: (b * nq + i, h)),
        out_shape=jax.ShapeDtypeStruct((t, BRANCH_WIDTH), BF16),
        scratch_shapes=[pltpu.VMEM((seq, hw), BF16), pltpu.VMEM((seq, hw), BF16)],
        compiler_params=_cparams(("parallel", "parallel", "arbitrary")),
        name="stickbreak",
    )(proj, proj, proj, q_gain.reshape(1, SB_DH), k_gain.reshape(1, SB_DH))


def _merge_kernel(ya_ref, yb_ref, yc_ref, wb_ref, ga_ref, gb_ref, gc_ref, o_ref):
    acc = _sigmoid(ga_ref[...]) * _dot(ya_ref[...], wb_ref[0])
    acc = acc + _sigmoid(gb_ref[...]) * _dot(yb_ref[...], wb_ref[1])
    acc = acc + _sigmoid(gc_ref[...]) * _dot(yc_ref[...], wb_ref[2])
    o_ref[...] = acc.astype(o_ref.dtype)


def _merge(ya, yb, yc, proj, w_branch, *, tm=1024, tn=512):
    t = ya.shape[0]
    tm = min(tm, t)
    nj = D_MODEL // tn
    g0 = COL_GATES * LANES // tn
    ysp = pl.BlockSpec((tm, BRANCH_WIDTH), lambda i, j: (i, 0))

    def gate(nb):
        return pl.BlockSpec((tm, tn), lambda i, j: (i, g0 + nb * nj + j))

    return pl.pallas_call(
        _merge_kernel,
        grid=(t // tm, nj),
        in_specs=[ysp, ysp, ysp,
                  pl.BlockSpec((N_BRANCH, BRANCH_WIDTH, tn), lambda i, j: (0, 0, j)),
                  gate(0), gate(1), gate(2)],
        out_specs=pl.BlockSpec((tm, tn), lambda i, j: (i, j)),
        out_shape=jax.ShapeDtypeStruct((t, D_MODEL), BF16),
        compiler_params=_cparams(("parallel", "arbitrary")),
        name="merge",
    )(ya, yb, yc, w_branch, proj, proj, proj)


def _out_proj_kernel(m_ref, w_ref, x_ref, g_ref, x1_ref, xn_ref):
    x1 = x_ref[...] + _dot(m_ref[...], w_ref[...])
    x1_ref[...] = x1
    ms = jnp.mean(x1 * x1, axis=-1, keepdims=True)
    xn_ref[...] = (x1 * lax.rsqrt(ms + EPS) * g_ref[...]).astype(BF16)


def _out_proj(merged, w_out, x2d, gain, *, tm=256):
    t, d = x2d.shape
    tm = min(tm, t)
    rows = pl.BlockSpec((tm, d), lambda i: (i, 0))
    return pl.pallas_call(
        _out_proj_kernel,
        grid=(t // tm,),
        in_specs=[rows, pl.BlockSpec((d, d), lambda i: (0, 0)), rows,
                  pl.BlockSpec((1, d), lambda i: (0, 0))],
        out_specs=[rows, rows],
        out_shape=[jax.ShapeDtypeStruct((t, d), F32), jax.ShapeDtypeStruct((t, d), BF16)],
        compiler_params=_cparams(("parallel",)),
        name="out_proj",
    )(merged, w_out, x2d, gain.reshape(1, d))


FFN_RB = 512
FFN_FC = 512
MOE_FC = 1024


FFN_ROW_STEP = 128


def _ffn_kernel(be_ref, used_ref, x_ref, w1_ref, w3_ref, w2_ref, *rest, residual, ragged, n_cast):
    rest = list(rest)
    res_ref = rest.pop(0) if residual else None
    cast_in = [rest.pop(0) for _ in range(n_cast)]
    o_ref = rest.pop(0)
    cast_out = [rest.pop(0) for _ in range(n_cast)]
    acc_ref, = rest
    i, c = pl.program_id(0), pl.program_id(1)
    rb = x_ref.shape[0]
    used = used_ref[i]

    for src, dst in zip(cast_in, cast_out):
        dst[...] = src[...].astype(dst.dtype)

    @pl.when(c == 0)
    def _():
        acc_ref[...] = jnp.zeros_like(acc_ref)

    def accumulate(nrows):
        x = x_ref[0:nrows, :]
        h1 = _dot(x, w1_ref[...])
        h3 = _dot(x, w3_ref[...])
        acc_ref[0:nrows, :] += _dot((_silu(h1) * h3).astype(BF16), w2_ref[...])

    if ragged:
        for nrows in range(FFN_ROW_STEP, rb + 1, FFN_ROW_STEP):
            pl.when(used == nrows)(functools.partial(accumulate, nrows))
    else:
        pl.when(used > 0)(functools.partial(accumulate, rb))

    @pl.when(c == pl.num_programs(1) - 1)
    def _():
        y = acc_ref[...]
        if residual:
            y = y + res_ref[...]
        o_ref[...] = y.astype(o_ref.dtype)


def _ffn(x_rows, w1, w3, w2, block_expert, block_used, residual=None, out_dtype=F32, fc=FFN_FC,
         ragged=False, cast=()):
    r, d = x_rows.shape
    f = w1.shape[-1]
    rb = min(FFN_RB, r)
    nblk, nc = r // rb, f // fc
    last = nc - 1

    def ccol(c, used):
        return jnp.where(used > 0, c, last)

    rows = pl.BlockSpec((rb, d), lambda i, c, be, us: (i, 0))
    in_specs = [
        rows,
        pl.BlockSpec((None, d, fc), lambda i, c, be, us: (be[i], 0, ccol(c, us[i]))),
        pl.BlockSpec((None, d, fc), lambda i, c, be, us: (be[i], 0, ccol(c, us[i]))),
        pl.BlockSpec((None, fc, d), lambda i, c, be, us: (be[i], ccol(c, us[i]), 0)),
    ]
    args = [x_rows, w1, w3, w2]
    if residual is not None:
        in_specs.append(rows)
        args.append(residual)
    cast_args, cast_specs, cast_shapes = _cast_specs(cast, nblk, nc)
    outs = pl.pallas_call(
        functools.partial(_ffn_kernel, residual=residual is not None, ragged=ragged, n_cast=len(cast)),
        grid_spec=pltpu.PrefetchScalarGridSpec(
            num_scalar_prefetch=2,
            grid=(nblk, nc),
            in_specs=in_specs + cast_specs,
            out_specs=[rows] + cast_specs,
            scratch_shapes=[pltpu.VMEM((rb, d), F32)],
        ),
        out_shape=[jax.ShapeDtypeStruct((r, d), out_dtype)] + cast_shapes,
        compiler_params=_cparams(("parallel", "arbitrary")),
        name="swiglu",
    )(block_expert, block_used, *args, *cast_args)
    return outs[0], [o.reshape(w.shape) for o, w in zip(outs[1:], cast)]


ROUTE_TT = 512
RT_W, RT_RANK, RT_MEMBER = 0, 8, 16


def _router_kernel(x_ref, g_ref, wr_ref, rowtab_ref, coltab_ref, cum_ref, count_ref):
    i = pl.program_id(0)
    tt = x_ref.shape[0]

    @pl.when(i == 0)
    def _():
        count_ref[...] = jnp.zeros_like(count_ref)

    x = x_ref[...]
    ms = jnp.mean(x * x, axis=-1, keepdims=True)
    xn = x * lax.rsqrt(ms + EPS) * g_ref[...]
    logits = jnp.dot(xn, wr_ref[...], preferred_element_type=F32, precision=lax.Precision.HIGHEST)
    lg = logits.T[0:N_EXPERTS, :]
    eidx = lax.broadcasted_iota(jnp.int32, lg.shape, 0).astype(F32)
    m1 = jnp.max(lg, axis=0, keepdims=True)
    i1 = jnp.min(jnp.where(lg == m1, eidx, float(N_EXPERTS)), axis=0, keepdims=True)
    sel1 = eidx == i1
    lg2 = jnp.where(sel1, -jnp.inf, lg)
    m2 = jnp.max(lg2, axis=0, keepdims=True)
    i2 = jnp.min(jnp.where(lg2 == m2, eidx, float(N_EXPERTS)), axis=0, keepdims=True)
    sel2 = eidx == i2
    e2 = jnp.exp(m2 - m1)
    den = 1.0 + e2
    wts = jnp.where(sel1, 1.0 / den, 0.0) + jnp.where(sel2, e2 / den, 0.0)
    member = (sel1 | sel2).astype(F32)

    srow = lax.broadcasted_iota(jnp.int32, (tt, tt), 0)
    scol = lax.broadcasted_iota(jnp.int32, (tt, tt), 1)
    before = (srow < scol).astype(BF16)
    base = count_ref[...]
    rank = _dot(member.astype(BF16), before) + base
    cum_ref[...] = jnp.broadcast_to(base, (N_EXPERTS, LANES))[None]
    count_ref[...] = base + jnp.sum(member, axis=1, keepdims=True)

    rowtab = jnp.concatenate([wts, rank, member], axis=0)
    rowtab_ref[...] = rowtab
    pad = jnp.zeros((LANES - 3 * N_EXPERTS, tt), F32)
    coltab_ref[...] = jnp.concatenate([rowtab, pad], axis=0).T


def _router(x1, gain, w_router):
    t, d = x1.shape
    tt = min(ROUTE_TT, t)
    nt = t // tt
    wr = jnp.zeros((d, LANES), F32).at[:, :N_EXPERTS].set(w_router.astype(F32))
    return pl.pallas_call(
        _router_kernel,
        grid=(nt,),
        in_specs=[pl.BlockSpec((tt, d), lambda i: (i, 0)),
                  pl.BlockSpec((1, d), lambda i: (0, 0)),
                  pl.BlockSpec((d, LANES), lambda i: (0, 0))],
        out_specs=[pl.BlockSpec((3 * N_EXPERTS, tt), lambda i: (0, i)),
                   pl.BlockSpec((tt, LANES), lambda i: (i, 0)),
                   pl.BlockSpec((1, N_EXPERTS, LANES), lambda i: (i, 0, 0))],
        out_shape=[jax.ShapeDtypeStruct((3 * N_EXPERTS, t), F32),
                   jax.ShapeDtypeStruct((t, LANES), F32),
                   jax.ShapeDtypeStruct((nt, N_EXPERTS, LANES), F32)],
        scratch_shapes=[pltpu.VMEM((N_EXPERTS, 1), F32)],
        compiler_params=_cparams(("arbitrary",)),
        name="router",
    )(x1, gain.reshape(1, d), wr)


PAIR_WIN = 256


def _gather_kernel(pb_ref, pt_ref, pe_ref, plo_ref, pfirst_ref, pskip_ref, pwin_ref, x_ref, rt_ref,
                   o_ref):
    g = pl.program_id(0)
    rb, tt = o_ref.shape[0], x_ref.shape[0]
    win = pwin_ref[g]

    @pl.when(pfirst_ref[g] == 1)
    def _():
        o_ref[...] = jnp.zeros_like(o_ref)

    def target_row():
        e = pe_ref[g]
        rt = rt_ref[...]
        ridx = lax.broadcasted_iota(jnp.int32, rt.shape, 0)
        rank = jnp.sum(jnp.where(ridx == RT_RANK + e, rt, 0.0), axis=0, keepdims=True)
        member = jnp.sum(jnp.where(ridx == RT_MEMBER + e, rt, 0.0), axis=0, keepdims=True)
        return jnp.where(member > 0.5, rank - plo_ref[g].astype(F32), -1.0)

    @pl.when((pskip_ref[g] == 0) & (win >= 0))
    def _():
        w0 = pl.multiple_of(win, LANES)
        rows = (lax.broadcasted_iota(jnp.int32, (PAIR_WIN, tt), 0) + w0).astype(F32)
        onehot = (rows == target_row()).astype(BF16)
        o_ref[pl.ds(w0, PAIR_WIN), :] += _dot(onehot, x_ref[...]).astype(o_ref.dtype)

    @pl.when((pskip_ref[g] == 0) & (win < 0))
    def _():
        rows = lax.broadcasted_iota(jnp.int32, (rb, tt), 0).astype(F32)
        onehot = (rows == target_row()).astype(BF16)
        o_ref[...] += _dot(onehot, x_ref[...]).astype(o_ref.dtype)


def _gather_rows(xn, rowtab, pairs, nb):
    t, d = xn.shape
    tt = min(ROUTE_TT, t)
    rb = FFN_RB
    npairs = pairs[0].shape[0]
    return pl.pallas_call(
        _gather_kernel,
        grid_spec=pltpu.PrefetchScalarGridSpec(
            num_scalar_prefetch=7,
            grid=(npairs,),
            in_specs=[pl.BlockSpec((tt, d), lambda g, pb, pt, *_: (pt[g], 0)),
                      pl.BlockSpec((3 * N_EXPERTS, tt), lambda g, pb, pt, *_: (0, pt[g]))],
            out_specs=pl.BlockSpec((rb, d), lambda g, pb, *_: (pb[g], 0)),
        ),
        out_shape=jax.ShapeDtypeStruct((nb * rb, d), BF16),
        compiler_params=_cparams(("arbitrary",)),
        name="moe_gather",
    )(*pairs, xn, rowtab)


def _combine_kernel(pb_ref, pt_ref, pe_ref, plo_ref, pfirst_ref, pskip_ref, pwin_ref, y_ref, ct_ref,
                    x_ref, o_ref):
    g = pl.program_id(0)
    tt, rb = o_ref.shape[0], y_ref.shape[0]
    win = pwin_ref[g]

    @pl.when(pfirst_ref[g] == 1)
    def _():
        o_ref[...] = x_ref[...]

    def weight_and_target():
        e = pe_ref[g]
        ct = ct_ref[...]
        lane = lax.broadcasted_iota(jnp.int32, ct.shape, 1)
        wcol = jnp.sum(jnp.where(lane == RT_W + e, ct, 0.0), axis=1, keepdims=True)
        rank = jnp.sum(jnp.where(lane == RT_RANK + e, ct, 0.0), axis=1, keepdims=True)
        member = jnp.sum(jnp.where(lane == RT_MEMBER + e, ct, 0.0), axis=1, keepdims=True)
        return wcol, jnp.where(member > 0.5, rank - plo_ref[g].astype(F32), -1.0)

    @pl.when((pskip_ref[g] == 0) & (win >= 0))
    def _():
        w0 = pl.multiple_of(win, LANES)
        wcol, target = weight_and_target()
        cols = (lax.broadcasted_iota(jnp.int32, (tt, PAIR_WIN), 1) + w0).astype(F32)
        onehot = (cols == target).astype(BF16)
        o_ref[...] += wcol * _dot(onehot, y_ref[pl.ds(w0, PAIR_WIN), :])

    @pl.when((pskip_ref[g] == 0) & (win < 0))
    def _():
        wcol, target = weight_and_target()
        cols = lax.broadcasted_iota(jnp.int32, (tt, rb), 1).astype(F32)
        onehot = (cols == target).astype(BF16)
        o_ref[...] += wcol * _dot(onehot, y_ref[...])


def _combine(y_rows, coltab, x1, pairs):
    t, d = x1.shape
    tt = min(ROUTE_TT, t)
    rb = FFN_RB
    npairs = pairs[0].shape[0]
    return pl.pallas_call(
        _combine_kernel,
        grid_spec=pltpu.PrefetchScalarGridSpec(
            num_scalar_prefetch=7,
            grid=(npairs,),
            in_specs=[pl.BlockSpec((rb, d), lambda g, pb, *_: (pb[g], 0)),
                      pl.BlockSpec((tt, LANES), lambda g, pb, pt, *_: (pt[g], 0)),
                      pl.BlockSpec((tt, d), lambda g, pb, pt, *_: (pt[g], 0))],
            out_specs=pl.BlockSpec((tt, d), lambda g, pb, pt, *_: (pt[g], 0)),
        ),
        out_shape=jax.ShapeDtypeStruct((t, d), F32),
        compiler_params=_cparams(("arbitrary",)),
        name="moe_combine",
    )(*pairs, y_rows, coltab, x1)


def _pair_tables(cum, counts, nb, order):
    nt = cum.shape[0]
    rb = FFN_RB
    npairs = nb + N_EXPERTS * nt
    padded = (counts + rb - 1) // rb * rb
    seg_end = jnp.cumsum(padded)
    seg_start = seg_end - padded
    blk = jnp.arange(nb, dtype=jnp.int32)
    blk_row = blk * rb
    blk_e = jnp.minimum(jnp.searchsorted(seg_end, blk_row, side='right'), N_EXPERTS - 1).astype(jnp.int32)
    blk_lo = blk_row - seg_start[blk_e]
    blk_valid = (blk_row < seg_end[-1]) & (blk_lo < counts[blk_e])
    blk_hi = jnp.minimum(blk_lo + rb, counts[blk_e])
    tile_lo = cum[:, blk_e]
    tile_hi = jnp.concatenate([cum[1:], counts[None, :]], axis=0)[:, blk_e]
    overlap = blk_valid[None, :] & (tile_lo < blk_hi[None, :]) & (tile_hi > blk_lo[None, :]) \
        & (tile_hi > tile_lo)
    if order == 'block':
        flat = overlap.T.reshape(-1)
        idx = jnp.nonzero(flat, size=npairs, fill_value=-1)[0].astype(jnp.int32)
        n_valid = jnp.sum(flat.astype(jnp.int32))
        last = idx[jnp.maximum(n_valid - 1, 0)]
        idx = jnp.where(idx < 0, last, idx)
        pb, pt = idx // nt, idx % nt
        major = pb
    else:
        flat = overlap.reshape(-1)
        idx = jnp.nonzero(flat, size=npairs, fill_value=-1)[0].astype(jnp.int32)
        n_valid = jnp.sum(flat.astype(jnp.int32))
        last = idx[jnp.maximum(n_valid - 1, 0)]
        idx = jnp.where(idx < 0, last, idx)
        pt, pb = idx // nb, idx % nb
        major = pt
    pos = jnp.arange(npairs, dtype=jnp.int32)
    skip = (pos >= n_valid).astype(jnp.int32)
    first = jnp.concatenate([jnp.ones((1,), jnp.int32),
                             (major[1:] != major[:-1]).astype(jnp.int32)]) * (1 - skip)
    r0 = jnp.maximum(tile_lo[pt, pb] - blk_lo[pb], 0)
    r1 = jnp.minimum(tile_hi[pt, pb], blk_hi[pb]) - blk_lo[pb]
    w0 = jnp.minimum(r0 // LANES * LANES, rb - PAIR_WIN)
    win = jnp.where(r1 <= w0 + PAIR_WIN, w0, -1)
    pairs = (pb.astype(jnp.int32), pt.astype(jnp.int32), blk_e[pb], blk_lo[pb].astype(jnp.int32),
             first.astype(jnp.int32), skip, win.astype(jnp.int32))
    step = FFN_ROW_STEP
    blk_used = jnp.where(blk_valid, (blk_hi - blk_lo + step - 1) // step * step, 0)
    return pairs, blk_e, blk_used.astype(jnp.int32)


def _moe(x1, xn, gain, w_router, w1, w3, w2):
    t, d = x1.shape
    rowtab, coltab, cum3 = _router(x1, gain, w_router)
    cum = cum3[:, :, 0].astype(jnp.int32)
    tt = min(ROUTE_TT, t)
    last_members = jnp.sum(rowtab[RT_MEMBER:RT_MEMBER + N_EXPERTS, t - tt:], axis=1).astype(jnp.int32)
    counts = cum[-1] + last_members
    nb = (t * TOP_K) // FFN_RB + N_EXPERTS
    pairs_g, blk_e, blk_used = _pair_tables(cum, counts, nb, 'block')
    pairs_c, _, _ = _pair_tables(cum, counts, nb, 'tile')
    x_rows = _gather_rows(xn, rowtab, pairs_g, nb)
    y_rows, _ = _ffn(x_rows, w1, w3, w2, blk_e, blk_used, out_dtype=BF16, fc=MOE_FC, ragged=True)
    return _combine(y_rows, coltab, x1, pairs_c)


PREP_TN = 1024
PREP_GROUPS = ((0, 4, 0), (4, 7, 16), (7, 16, 32))
PREP_SMALL = 16
PREP_SMALL_AT = (4096, 7184)
PREP_TAIL = 32


def _w_in_prep_kernel(a_ref, b_ref, sa_ref, sb_ref, main_ref, small_ref):
    j = pl.program_id(0)
    x = jnp.concatenate([a_ref[...], b_ref[...]], axis=0)
    for first, end, shift in PREP_GROUPS:
        @pl.when((j >= first) & (j < end))
        def _():
            main_ref[...] = x[shift:shift + PREP_TN, :].astype(BF16)

    @pl.when(j == 0)
    def _():
        pad = jnp.zeros((LANES - 2 * PREP_SMALL, sa_ref.shape[1]), F32)
        small_ref[...] = jnp.concatenate([sa_ref[...], sb_ref[...], pad], axis=0).astype(BF16)


def _split_w_in(w_all, layer):
    wt = jnp.swapaxes(w_all, 1, 2)
    _, n_in, d = wt.shape
    return pl.pallas_call(
        _w_in_prep_kernel,
        grid=(N_MAIN // PREP_TN,),
        in_specs=[
            pl.BlockSpec((None, PREP_TN, d), lambda j: (layer, j, 0)),
            pl.BlockSpec((None, PREP_TAIL, d), lambda j: (layer, (j + 1) * (PREP_TN // PREP_TAIL), 0)),
            pl.BlockSpec((None, PREP_SMALL, d), lambda j: (layer, PREP_SMALL_AT[0] // PREP_SMALL, 0)),
            pl.BlockSpec((None, PREP_SMALL, d), lambda j: (layer, PREP_SMALL_AT[1] // PREP_SMALL, 0)),
        ],
        out_specs=[pl.BlockSpec((PREP_TN, d), lambda j: (j, 0)),
                   pl.BlockSpec((LANES, d), lambda j: (0, 0))],
        out_shape=[jax.ShapeDtypeStruct((N_MAIN, d), BF16), jax.ShapeDtypeStruct((LANES, d), BF16)],
        compiler_params=_cparams(("arbitrary",)),
        name="w_in_prep",
    )(wt, wt, wt, wt)


def kernel(x, norm_mix, w_in, conv_w, gdn_a_log, gdn_dt_bias, gdn_norm, gla_w_gate, gla_b_gate,
           gla_norm, sb_q_norm, sb_k_norm, w_branch, w_out, norm_ffn, ffn_w1, ffn_w3, ffn_w2,
           moe_router, moe_w1, moe_w3, moe_w2):
    batch, seq, d = x.shape
    t = batch * seq
    depth = w_in.shape[0]
    x2 = x.reshape(t, d)
    nblk = t // min(FFN_RB, t)
    expert_bf16 = {}

    def ride(layer, name, w, n0, n1):
        ok = layer % 2 == 1 and _cast_blocks(w, n0, n1) is not None
        return ((layer, name), w) if ok else None

    for layer in range(depth):
        w_main, w_small = _split_w_in(w_in, layer)
        i = layer // 2
        if layer % 2 == 0 and layer + 1 < depth:
            job = ride(layer + 1, 'w1', moe_w1[i], *_in_proj_grid(t))
        elif layer % 2 == 1:
            job = None if (layer, 'w2') in expert_bf16 else ride(layer, 'w2', moe_w2[i], *_in_proj_grid(t))
        else:
            job = None
        proj, small, casted = _in_proj(x2, norm_mix[layer], w_main, w_small,
                                       cast=(job[1],) if job else ())
        if job:
            expert_bf16[job[0]] = casted[0]
        ya = _gdn(proj, small, conv_w[layer], gdn_a_log[layer], gdn_dt_bias[layer], gdn_norm[layer],
                  batch, seq)
        yb = _gla(proj, small, gla_w_gate[layer], gla_b_gate[layer], gla_norm[layer], batch, seq)
        yc = _sb(proj, sb_q_norm[layer], sb_k_norm[layer], batch, seq)
        merged = _merge(ya, yb, yc, proj, w_branch[layer].astype(BF16))
        x1, xn = _out_proj(merged, w_out[layer].astype(BF16), x2, norm_ffn[layer])
        if layer % 2 == 0:
            job = ride(layer + 1, 'w3', moe_w3[i], nblk, D_FF // FFN_FC) if layer + 1 < depth else None
            x2, casted = _ffn(xn, ffn_w1[i][None].astype(BF16), ffn_w3[i][None].astype(BF16),
                              ffn_w2[i][None].astype(BF16), jnp.zeros((nblk,), jnp.int32),
                              jnp.full((nblk,), min(FFN_RB, t), jnp.int32), residual=x1,
                              cast=(job[1],) if job else ())
            if job:
                expert_bf16[job[0]] = casted[0]
        else:
            w1, w3, w2 = (expert_bf16.pop((layer, name), None) for name in ('w1', 'w3', 'w2'))
            w1 = moe_w1[i].astype(BF16) if w1 is None else w1
            w3 = moe_w3[i].astype(BF16) if w3 is None else w3
            w2 = moe_w2[i].astype(BF16) if w2 is None else w2
            x2 = _moe(x1, xn, norm_ffn[layer], moe_router[i], w1, w3, w2)
    return x2.reshape(batch, seq, d)
```
